```python
import math
import jax, jax.numpy as jnp
from jax import lax
import numpy as np

D_MODEL = 1024
BATCH = 4
SEQ = 4096
DEPTH = 1

N_MEM = 256
HA = 8
DA = 64
A_WIDTH = HA * 2 * DA
Q_BLOCK = 128
NUM_BUCKETS = 32
MAX_DISTANCE = 128
HB = 8
KB = 128
VB = 128
B_WIDTH = HB * VB
CHUNK = 64
HC = 4
DC = 256
C_WIDTH = HC * DC
N_BRANCH = 3
IN_SIZES = (
    2 * HA * DA, 2 * HA * DA, A_WIDTH, A_WIDTH,
    HB * KB, HB * KB, HB * KB, B_WIDTH, B_WIDTH,
    C_WIDTH, C_WIDTH,
    N_BRANCH * D_MODEL,
)
IN_COLS = sum(IN_SIZES)
EPS = 1e-6

kernel_name = "hybrid_diffattn_hgrn2_memxattn_block"


def rmsnorm(x, w):
    xf = x.astype(jnp.float32)
    y = xf * lax.rsqrt(jnp.mean(xf * xf, axis=-1, keepdims=True) + EPS)
    return (y * w.astype(jnp.float32)).astype(x.dtype)


def t5_bucket(rel):
    nb = NUM_BUCKETS // 2
    max_exact = nb // 2
    ret = jnp.where(rel > 0, nb, 0)
    n = jnp.abs(rel)
    nf = jnp.maximum(n, 1).astype(jnp.float32)
    large = max_exact + (jnp.log(nf / max_exact) / math.log(MAX_DISTANCE / max_exact)
                         * (nb - max_exact)).astype(jnp.int32)
    large = jnp.minimum(large, nb - 1)
    return ret + jnp.where(n < max_exact, n, large)


def diff_attention(q, k, v, lam, rel_bias):
    B, L = q.shape[0], q.shape[1]
    nblk = L // Q_BLOCK
    qb = q.reshape(B, nblk, Q_BLOCK, 2, HA, DA).transpose(1, 0, 3, 4, 2, 5)
    kt = k.transpose(0, 2, 3, 1, 4)
    vt = v.transpose(0, 2, 1, 3)
    kpos = jnp.arange(L, dtype=jnp.int32)
    scale = DA ** -0.5

    def block(args):
        qblk, start = args
        qpos = start + jnp.arange(Q_BLOCK, dtype=jnp.int32)
        bucket = t5_bucket(kpos[None, :] - qpos[:, None])
        bias = jnp.transpose(rel_bias[bucket], (2, 0, 1)).astype(jnp.float32)
        s = jnp.einsum('bmhqd,bmhkd->bmhqk', qblk, kt).astype(jnp.float32) * scale + bias
        p = jax.nn.softmax(s, axis=-1)
        w = p[:, 0] - lam * p[:, 1]
        return jnp.einsum('bhqk,bhkv->bhqv', w.astype(vt.dtype), vt)

    starts = jnp.arange(nblk, dtype=jnp.int32) * Q_BLOCK
    out = lax.map(block, (qb, starts))
    return out.transpose(1, 0, 3, 2, 4).reshape(B, L, HA, 2 * DA)


def gla_chunk(q, k, v, g):
    B, H, L, K = q.shape
    V = v.shape[-1]
    N = L // CHUNK
    qc = q.reshape(B, H, N, CHUNK, K).astype(jnp.float32)
    kc = k.reshape(B, H, N, CHUNK, K).astype(jnp.float32)
    vc = v.reshape(B, H, N, CHUNK, V).astype(jnp.float32)
    gc = g.reshape(B, H, N, CHUNK, K).astype(jnp.float32)
    b = jnp.cumsum(gc, axis=3)
    b_mid = b[:, :, :, CHUNK // 2 - 1:CHUNK // 2, :]
    b_last = b[:, :, :, CHUNK - 1:, :]
    a = jnp.einsum('bhnck,bhnsk->bhncs', qc * jnp.exp(b - b_mid), kc * jnp.exp(b_mid - b))
    mask = jnp.arange(CHUNK)[None, :] <= jnp.arange(CHUNK)[:, None]
    a = jnp.where(mask, a, 0.0)
    o_intra = jnp.einsum('bhncs,bhnsv->bhncv', a, vc)
    u = jnp.einsum('bhnsk,bhnsv->bhnkv', kc * jnp.exp(b_last - b), vc)
    decay = jnp.exp(b_last[:, :, :, 0, :])

    def step(s, inp):
        d, un = inp
        return d[..., None] * s + un, s

    s0 = jnp.zeros((B, H, K, V), jnp.float32)
    _, s_prev = lax.scan(step, s0, (jnp.moveaxis(decay, 2, 0), jnp.moveaxis(u, 2, 0)))
    s_prev = jnp.moveaxis(s_prev, 0, 2)
    o_inter = jnp.einsum('bhnck,bhnkv->bhncv', qc * jnp.exp(b), s_prev)
    return (o_intra + o_inter).reshape(B, H, L, V)


def hgrn2_gates(f_logit, lb):
    f32 = f_logit.astype(jnp.float32)
    g = jnp.log(lb + (1.0 - lb) * jax.nn.sigmoid(f32))
    kk = (1.0 - lb) * jax.nn.sigmoid(-f32)
    return g, kk


def to_heads(t, h, d):
    B, L = t.shape[0], t.shape[1]
    return t.reshape(B, L, h, d).transpose(0, 2, 1, 3)


def split_columns(p):
    idx = []
    acc = 0
    for s in IN_SIZES[:-1]:
        acc += s
        idx.append(acc)
    return jnp.split(p, idx, axis=-1)


def setup_inputs(seed: int = 0) -> dict:
    key = jax.random.key(seed)
    ks = jax.random.split(key, 20)
    f32 = jnp.float32
    nrm = lambda k, shape, s: (jax.random.normal(k, shape, f32) * s)
    return {
        "x": nrm(ks[0], (BATCH, SEQ, D_MODEL), 1.0),
        "mem": nrm(ks[1], (BATCH, N_MEM, D_MODEL), 1.0),
        "pre_norm": 1.0 + nrm(ks[2], (DEPTH, D_MODEL), 0.02),
        "post_norm": 1.0 + nrm(ks[3], (DEPTH, D_MODEL), 0.02),
        "w_in": nrm(ks[4], (DEPTH, D_MODEL, IN_COLS), D_MODEL ** -0.5),
        "lambda_q1": nrm(ks[5], (DEPTH, DA), 0.1),
        "lambda_k1": nrm(ks[6], (DEPTH, DA), 0.1),
        "lambda_q2": nrm(ks[7], (DEPTH, DA), 0.1),
        "lambda_k2": nrm(ks[8], (DEPTH, DA), 0.1),
        "diff_subln": 1.0 + nrm(ks[9], (DEPTH, 2 * DA), 0.02),
        "rel_bias": nrm(ks[10], (NUM_BUCKETS, HA), 0.5),
        "lb_logits": nrm(ks[11], (2, DEPTH + 1, HB * KB), 0.1),
        "hgrn_norm": 1.0 + nrm(ks[12], (DEPTH, VB), 0.02),
        "mem_norm": 1.0 + nrm(ks[13], (DEPTH, D_MODEL), 0.02),
        "w_mem_kv": nrm(ks[14], (DEPTH, D_MODEL, 2 * C_WIDTH), D_MODEL ** -0.5),
        "w_branch": nrm(ks[15], (DEPTH, N_BRANCH, A_WIDTH, D_MODEL), A_WIDTH ** -0.5),
        "w_out": nrm(ks[16], (DEPTH, D_MODEL, D_MODEL), D_MODEL ** -0.5),
    }


def reference(x, mem, pre_norm, post_norm, w_in, lambda_q1, lambda_k1, lambda_q2, lambda_k2,
              diff_subln, rel_bias, lb_logits, hgrn_norm, mem_norm, w_mem_kv, w_branch, w_out):
    B, L, _ = x.shape
    lb_all = jnp.cumsum(jax.nn.softmax(lb_logits.astype(jnp.float32), axis=1), axis=1)
    for l in range(DEPTH):
        h = rmsnorm(x, pre_norm[l])
        proj = h @ w_in[l]
        aq, ak, av, az, bq, bff, bfb, bi, bz, cq, cz, gates = split_columns(proj)

        lam_init = 0.8 - 0.6 * math.exp(-0.3 * l)
        lam = (jnp.exp(jnp.sum(lambda_q1[l].astype(jnp.float32) * lambda_k1[l].astype(jnp.float32)))
               - jnp.exp(jnp.sum(lambda_q2[l].astype(jnp.float32) * lambda_k2[l].astype(jnp.float32)))
               + lam_init)
        oa = diff_attention(aq.reshape(B, L, 2, HA, DA), ak.reshape(B, L, 2, HA, DA),
                            av.reshape(B, L, HA, 2 * DA), lam, rel_bias)
        oa = rmsnorm(oa, diff_subln[l]) * (1.0 - lam_init)
        ya = (oa.reshape(B, L, A_WIDTH) * jax.nn.silu(az)) @ w_branch[l, 0]

        g_f, k_f = hgrn2_gates(bff, lb_all[0, l])
        g_b, k_b = hgrn2_gates(bfb, lb_all[1, l])
        qh = to_heads(bq, HB, KB)
        vh = to_heads(bi, HB, VB)
        o_fwd = gla_chunk(qh, to_heads(k_f, HB, KB), vh, to_heads(g_f, HB, KB))
        flip = lambda t: jnp.flip(t, axis=2)
        o_bwd = flip(gla_chunk(flip(qh), flip(to_heads(k_b, HB, KB)), flip(vh),
                               flip(to_heads(g_b, HB, KB))))
        ob = (o_fwd + o_bwd).transpose(0, 2, 1, 3)
        ob = rmsnorm(ob, hgrn_norm[l]).astype(x.dtype)
        yb = (ob.reshape(B, L, B_WIDTH) * jax.nn.silu(bz)) @ w_branch[l, 1]

        m = rmsnorm(mem, mem_norm[l])
        mk, mv = jnp.split(m @ w_mem_kv[l], 2, axis=-1)
        mk = mk.reshape(B, N_MEM, HC, DC)
        mv = mv.reshape(B, N_MEM, HC, DC)
        s = jnp.einsum('blhd,bmhd->bhlm', cq.reshape(B, L, HC, DC), mk).astype(jnp.float32) * (DC ** -0.5)
        p = jax.nn.softmax(s, axis=-1).astype(mv.dtype)
        oc = jnp.einsum('bhlm,bmhd->blhd', p, mv).reshape(B, L, C_WIDTH)
        yc = (oc * jax.nn.silu(cz)) @ w_branch[l, 2]

        ga, gb, gc = jnp.split(jax.nn.sigmoid(gates), N_BRANCH, axis=-1)
        y = (ga * ya + gb * yb + gc * yc) @ w_out[l]
        x = x + rmsnorm(y, post_norm[l])
    return x
```

```python
import functools
import math

import jax
import jax.numpy as jnp
import numpy as np
from jax import lax
from jax.experimental import pallas as pl
from jax.experimental.pallas import tpu as pltpu

F32 = jnp.float32
BF16 = jnp.bfloat16

D_MODEL = 1024
N_MEM = 256
HA, DA = 8, 64
HB, KB, VB = 8, 128, 128
HC, DC = 4, 256
NUM_BUCKETS, MAX_DISTANCE = 32, 128
CHUNK = 64
EPS = 1e-6
LANE = 128

COL_AQ, COL_AK, COL_AV, COL_AZ = 0, 1024, 2048, 3072
COL_BQ, COL_BFF, COL_BFB, COL_BI, COL_BZ = 4096, 5120, 6144, 7168, 8192
COL_CQ, COL_CZ, COL_G = 9216, 10240, 11264
IN_COLS = 14336

VMEM_LIMIT = 56 * 1024 * 1024


def _cparams(n_axes):
    return pltpu.CompilerParams(dimension_semantics=("arbitrary",) * n_axes,
                                vmem_limit_bytes=VMEM_LIMIT)


def _norm_matmul_kernel(x_ref, nw_ref, w_ref, o_ref, h_ref):
    @pl.when(pl.program_id(1) == 0)
    def _():
        x = x_ref[...]
        ms = jnp.mean(x * x, axis=-1, keepdims=True)
        h_ref[...] = (x * lax.rsqrt(ms + EPS) * nw_ref[...]).astype(BF16)

    o_ref[...] = jnp.dot(h_ref[...], w_ref[...], preferred_element_type=F32).astype(o_ref.dtype)


def _norm_matmul(x2d, norm_w, w_bf16, tm, tn):
    m, d = x2d.shape
    n = w_bf16.shape[1]
    return pl.pallas_call(
        _norm_matmul_kernel,
        grid=(m // tm, n // tn),
        in_specs=[pl.BlockSpec((tm, d), lambda i, j: (i, 0)),
                  pl.BlockSpec((1, d), lambda i, j: (0, 0)),
                  pl.BlockSpec((d, tn), lambda i, j: (0, j))],
        out_specs=pl.BlockSpec((tm, tn), lambda i, j: (i, j)),
        out_shape=jax.ShapeDtypeStruct((m, n), BF16),
        scratch_shapes=[pltpu.VMEM((tm, d), BF16)],
        compiler_params=_cparams(2),
        name="norm_matmul",
    )(x2d, norm_w, w_bf16)


def _attn_kernel(sc_ref, q_ref, k_ref, v_ref, z_ref, bias_ref, sub_ref, o_ref,
                 acc_ref, m_ref, l_ref, *, tq, tk, nk, out_scale):
    h = pl.program_id(1)
    i = pl.program_id(2)
    lam = sc_ref[0]
    c_left = sc_ref[1 + 2 * h]
    c_right = sc_ref[2 + 2 * h]

    q = q_ref[0] * jnp.asarray(DA ** -0.5, BF16)
    lane = lax.broadcasted_iota(jnp.int32, (tq, LANE), 1)
    zero = jnp.zeros_like(q)
    qq = jnp.concatenate([jnp.where(lane < DA, q, zero), jnp.where(lane >= DA, q, zero)], axis=0)

    m_ref[...] = jnp.full(m_ref.shape, -jnp.inf, F32)
    l_ref[...] = jnp.zeros(l_ref.shape, F32)
    acc_ref[...] = jnp.zeros(acc_ref.shape, F32)

    def tile(j, bias_tile, c):
        off = pl.multiple_of(j * tk, tk)
        k = k_ref[0, pl.ds(off, tk), :]
        v = v_ref[0, pl.ds(off, tk), :]
        s = lax.dot_general(qq, k, (((1,), (1,)), ((), ())), preferred_element_type=F32)
        if bias_tile is not None:
            s = (s.reshape(2, tq, tk) + bias_tile[None]).reshape(2 * tq, tk)
        m_prev = m_ref[...]
        m_cur = jnp.max(s, axis=1, keepdims=True)
        if c is not None:
            m_cur = m_cur + c
        m_new = jnp.maximum(m_prev, m_cur)
        alpha = jnp.exp(m_prev - m_new)
        shift = m_new if c is None else m_new - c
        p = jnp.exp(s - shift)
        l_ref[...] = alpha * l_ref[...] + jnp.sum(p, axis=1, keepdims=True)
        acc_ref[...] = alpha * acc_ref[...] + jnp.dot(p.astype(BF16), v, preferred_element_type=F32)
        m_ref[...] = m_new

    def far_left(j, carry):
        tile(j, None, c_left)
        return carry

    def near(j, carry):
        tile(j, bias_ref[0, j - i + 1], None)
        return carry

    def far_right(j, carry):
        tile(j, None, c_right)
        return carry

    lax.fori_loop(0, jnp.maximum(i - 1, 0), far_left, 0)
    lax.fori_loop(jnp.maximum(i - 1, 0), jnp.minimum(i + 2, nk), near, 0)
    lax.fori_loop(jnp.minimum(i + 2, nk), nk, far_right, 0)

    o = acc_ref[...] / l_ref[...]
    d = o[:tq] - lam * o[tq:]
    ms = jnp.mean(d * d, axis=-1, keepdims=True)
    y = d * lax.rsqrt(ms + EPS) * sub_ref[...] * out_scale
    z = z_ref[0].astype(F32)
    o_ref[0] = (y * (z * jax.nn.sigmoid(z))).astype(o_ref.dtype)


def _diff_attention(proj3, scalars, bias_tab, subln, t, out_scale):
    b, l, _ = proj3.shape
    nk = l // t
    cq, ck, cv, cz = COL_AQ // LANE, COL_AK // LANE, COL_AV // LANE, COL_AZ // LANE
    kern = functools.partial(_attn_kernel, tq=t, tk=t, nk=nk, out_scale=out_scale)
    return pl.pallas_call(
        kern,
        grid=(b, HA, nk),
        in_specs=[pl.BlockSpec(memory_space=pltpu.SMEM),
                  pl.BlockSpec((1, t, LANE), lambda b_, h, i: (b_, i, cq + h)),
                  pl.BlockSpec((1, l, LANE), lambda b_, h, i: (b_, 0, ck + h)),
                  pl.BlockSpec((1, l, LANE), lambda b_, h, i: (b_, 0, cv + h)),
                  pl.BlockSpec((1, t, LANE), lambda b_, h, i: (b_, i, cz + h)),
                  pl.BlockSpec((1, 3, t, t), lambda b_, h, i: (h, 0, 0, 0)),
                  pl.BlockSpec((1, LANE), lambda b_, h, i: (0, 0))],
        out_specs=pl.BlockSpec((1, t, LANE), lambda b_, h, i: (b_, i, h)),
        out_shape=jax.ShapeDtypeStruct((b, l, HA * 2 * DA), BF16),
        scratch_shapes=[pltpu.VMEM((2 * t, LANE), F32),
                        pltpu.VMEM((2 * t, 1), F32),
                        pltpu.VMEM((2 * t, 1), F32)],
        compiler_params=_cparams(3),
        name="diff_attention",
    )(scalars, proj3, proj3, proj3, proj3, bias_tab, subln)


HG_BLK = 256


def _hgrn_kernel(q_ref, ff_ref, fb_ref, v_ref, z_ref, lb_ref, nw_ref, tl_ref, tu_ref, o_ref,
                 qtf_ref, ktf_ref, khf_ref, qtb_ref, ktb_ref, khb_ref, qh_ref,
                 df_ref, db_ref, st_ref, sf_ref, sb_ref, *, l):
    n = l // CHUNK
    cpb = HG_BLK // CHUNK
    lb_f = lb_ref[0:1, :]
    lb_b = lb_ref[1:2, :]
    tl = tl_ref[...]
    tu = tu_ref[...]

    def cumsum(tri, g):
        g_hi = g.astype(BF16)
        g_lo = (g - g_hi.astype(F32)).astype(BF16)
        return (jnp.dot(tri, g_hi, preferred_element_type=F32)
                + jnp.dot(tri, g_lo, preferred_element_type=F32))

    def prep(r, carry):
        off = pl.multiple_of(r * HG_BLK, HG_BLK)
        rows = pl.ds(off, HG_BLK)
        q = q_ref[0, rows, :].astype(F32)

        def one_dir(f_ref, lb, tri, mid_row, last_row):
            x = f_ref[0, rows, :].astype(F32)
            g = jnp.log(lb + (1.0 - lb) * jax.nn.sigmoid(x))
            kk = (1.0 - lb) * jax.nn.sigmoid(-x)
            bcs = cumsum(tri, g).reshape(cpb, CHUNK, LANE)
            b_mid = bcs[:, mid_row:mid_row + 1, :]
            b_last = bcs[:, last_row:last_row + 1, :]
            q3 = q.reshape(cpb, CHUNK, LANE)
            k3 = kk.reshape(cpb, CHUNK, LANE)
            qt = (q3 * jnp.exp(bcs - b_mid)).reshape(HG_BLK, LANE)
            kt = (k3 * jnp.exp(b_mid - bcs)).reshape(HG_BLK, LANE)
            kh = (k3 * jnp.exp(b_last - bcs)).reshape(HG_BLK, LANE)
            qh = (q3 * jnp.exp(bcs)).reshape(HG_BLK, LANE)
            return qt, kt, kh, qh, jnp.exp(b_last)

        qt, kt, kh, qh_f, dec = one_dir(ff_ref, lb_f, tl, CHUNK // 2 - 1, CHUNK - 1)
        qtf_ref[rows, :] = qt.astype(BF16)
        ktf_ref[rows, :] = kt.astype(BF16)
        khf_ref[rows, :] = kh.astype(BF16)
        df_ref[pl.ds(r * cpb, cpb)] = dec
        qt, kt, kh, qh_b, dec = one_dir(fb_ref, lb_b, tu, CHUNK // 2, 0)
        qtb_ref[rows, :] = qt.astype(BF16)
        ktb_ref[rows, :] = kt.astype(BF16)
        khb_ref[rows, :] = kh.astype(BF16)
        db_ref[pl.ds(r * cpb, cpb)] = dec
        qh_ref[rows, :] = jnp.concatenate([qh_f, qh_b], axis=1).astype(BF16)
        return carry

    lax.fori_loop(0, l // HG_BLK, prep, 0)

    sf_ref[...] = jnp.zeros(sf_ref.shape, F32)
    sb_ref[...] = jnp.zeros(sb_ref.shape, F32)
    tn = (((0,), (0,)), ((), ()))

    def scan(t, carry):
        cf = pl.ds(pl.multiple_of(t * CHUNK, CHUNK), CHUNK)
        tb = n - 1 - t
        cb = pl.ds(pl.multiple_of(tb * CHUNK, CHUNK), CHUNK)
        s_f = sf_ref[...]
        s_b = sb_ref[...]
        st_ref[t, :, 0:KB] = s_f.astype(BF16)
        st_ref[tb, :, KB:2 * KB] = s_b.astype(BF16)
        u_f = lax.dot_general(v_ref[0, cf, :], khf_ref[cf, :], tn, preferred_element_type=F32)
        u_b = lax.dot_general(v_ref[0, cb, :], khb_ref[cb, :], tn, preferred_element_type=F32)
        sf_ref[...] = s_f * df_ref[t] + u_f
        sb_ref[...] = s_b * db_ref[tb] + u_b
        return carry

    lax.fori_loop(0, n, scan, 0)

    row = lax.broadcasted_iota(jnp.int32, (CHUNK, CHUNK), 0)
    col = lax.broadcasted_iota(jnp.int32, (CHUNK, CHUNK), 1)
    nt = (((1,), (1,)), ((), ()))
    nw = nw_ref[...]

    def out(c, carry):
        rows = pl.ds(pl.multiple_of(c * CHUNK, CHUNK), CHUNK)
        a_f = lax.dot_general(qtf_ref[rows, :], ktf_ref[rows, :], nt, preferred_element_type=F32)
        a_b = lax.dot_general(qtb_ref[rows, :], ktb_ref[rows, :], nt, preferred_element_type=F32)
        a = jnp.where(col <= row, a_f, 0.0) + jnp.where(col >= row, a_b, 0.0)
        o = (jnp.dot(a.astype(BF16), v_ref[0, rows, :], preferred_element_type=F32)
             + lax.dot_general(qh_ref[rows, :], st_ref[c], nt, preferred_element_type=F32))
        ms = jnp.mean(o * o, axis=-1, keepdims=True)
        y = o * lax.rsqrt(ms + EPS) * nw
        z = z_ref[0, rows, :].astype(F32)
        o_ref[0, rows, :] = (y * (z * jax.nn.sigmoid(z))).astype(o_ref.dtype)
        return carry

    lax.fori_loop(0, n, out, 0)


def _hgrn(proj3, lb2, hnorm, tri_l, tri_u):
    b, l, _ = proj3.shape
    n = l // CHUNK
    c = lambda col: col // LANE
    seq = lambda col: pl.BlockSpec((1, l, LANE), lambda b_, h: (b_, 0, c(col) + h))
    return pl.pallas_call(
        functools.partial(_hgrn_kernel, l=l),
        grid=(b, HB),
        in_specs=[seq(COL_BQ), seq(COL_BFF), seq(COL_BFB), seq(COL_BI), seq(COL_BZ),
                  pl.BlockSpec((2, LANE), lambda b_, h: (0, h)),
                  pl.BlockSpec((1, LANE), lambda b_, h: (0, 0)),
                  pl.BlockSpec((HG_BLK, HG_BLK), lambda b_, h: (0, 0)),
                  pl.BlockSpec((HG_BLK, HG_BLK), lambda b_, h: (0, 0))],
        out_specs=pl.BlockSpec((1, l, LANE), lambda b_, h: (b_, 0, h)),
        out_shape=jax.ShapeDtypeStruct((b, l, HB * VB), BF16),
        scratch_shapes=[pltpu.VMEM((l, KB), BF16)] * 6
                       + [pltpu.VMEM((l, 2 * KB), BF16),
                          pltpu.VMEM((n, 1, KB), F32), pltpu.VMEM((n, 1, KB), F32),
                          pltpu.VMEM((n, VB, 2 * KB), BF16),
                          pltpu.VMEM((VB, KB), F32), pltpu.VMEM((VB, KB), F32)],
        compiler_params=_cparams(2),
        name="hgrn2",
    )(proj3, proj3, proj3, proj3, proj3, lb2, hnorm, tri_l, tri_u)


def _mem_attn_kernel(q_ref, k_ref, v_ref, z_ref, o_ref):
    q = q_ref[0] * jnp.asarray(DC ** -0.5, BF16)
    s = lax.dot_general(q, k_ref[0], (((1,), (1,)), ((), ())), preferred_element_type=F32)
    m = jnp.max(s, axis=-1, keepdims=True)
    e = jnp.exp(s - m)
    p = e / jnp.sum(e, axis=-1, keepdims=True)
    o = jnp.dot(p.astype(BF16), v_ref[0], preferred_element_type=F32)
    z = z_ref[0].astype(F32)
    o_ref[0] = (o * (z * jax.nn.sigmoid(z))).astype(o_ref.dtype)


def _mem_attention(proj3, mkv3, tq):
    b, l, _ = proj3.shape
    cq, cz = COL_CQ // DC, COL_CZ // DC
    return pl.pallas_call(
        _mem_attn_kernel,
        grid=(b, HC, l // tq),
        in_specs=[pl.BlockSpec((1, tq, DC), lambda b_, h, i: (b_, i, cq + h)),
                  pl.BlockSpec((1, N_MEM, DC), lambda b_, h, i: (b_, 0, h)),
                  pl.BlockSpec((1, N_MEM, DC), lambda b_, h, i: (b_, 0, HC + h)),
                  pl.BlockSpec((1, tq, DC), lambda b_, h, i: (b_, i, cz + h))],
        out_specs=pl.BlockSpec((1, tq, DC), lambda b_, h, i: (b_, i, h)),
        out_shape=jax.ShapeDtypeStruct((b, l, HC * DC), BF16),
        compiler_params=_cparams(3),
        name="mem_attention",
    )(proj3, mkv3, mkv3, proj3)


def _merge_kernel(x_ref, ua_ref, ub_ref, uc_ref, ga_ref, gb_ref, gc_ref, wb_ref, wo_ref, pn_ref, o_ref):
    def branch(u_ref, g_ref, idx):
        y = jnp.dot(u_ref[...], wb_ref[idx], preferred_element_type=F32)
        return jax.nn.sigmoid(g_ref[...].astype(F32)) * y

    merged = branch(ua_ref, ga_ref, 0) + branch(ub_ref, gb_ref, 1) + branch(uc_ref, gc_ref, 2)
    y = jnp.dot(merged.astype(BF16), wo_ref[...], preferred_element_type=F32)
    ms = jnp.mean(y * y, axis=-1, keepdims=True)
    o_ref[...] = x_ref[...] + y * lax.rsqrt(ms + EPS) * pn_ref[...]


def _merge(x2d, ua, ub, uc, proj2, wb, wo, pn, tm):
    m, d = x2d.shape
    g0 = COL_G // d
    row = lambda: pl.BlockSpec((tm, d), lambda i: (i, 0))
    gate = lambda k: pl.BlockSpec((tm, d), lambda i: (i, g0 + k))
    return pl.pallas_call(
        _merge_kernel,
        grid=(m // tm,),
        in_specs=[row(), row(), row(), row(), gate(0), gate(1), gate(2),
                  pl.BlockSpec((3, d, d), lambda i: (0, 0, 0)),
                  pl.BlockSpec((d, d), lambda i: (0, 0)),
                  pl.BlockSpec((1, d), lambda i: (0, 0))],
        out_specs=row(),
        out_shape=jax.ShapeDtypeStruct((m, d), F32),
        compiler_params=_cparams(1),
        name="merge_out",
    )(x2d, ua, ub, uc, proj2, proj2, proj2, wb, wo, pn)


def _t5_bucket(rel):
    nb = NUM_BUCKETS // 2
    max_exact = nb // 2
    ret = jnp.where(rel > 0, nb, 0)
    n = jnp.abs(rel)
    nf = jnp.maximum(n, 1).astype(jnp.float32)
    large = max_exact + (jnp.log(nf / max_exact) / math.log(MAX_DISTANCE / max_exact)
                         * (nb - max_exact)).astype(jnp.int32)
    large = jnp.minimum(large, nb - 1)
    return ret + jnp.where(n < max_exact, n, large)


def _bias_tables(rel_bias, t):
    r = jnp.arange(t, dtype=jnp.int32)
    d = jnp.asarray([-t, 0, t], jnp.int32)
    rel = d[:, None, None] + r[None, None, :] - r[None, :, None]
    tab = rel_bias.astype(F32)[_t5_bucket(rel)]
    far = rel_bias.astype(F32)[_t5_bucket(jnp.asarray([-MAX_DISTANCE, MAX_DISTANCE], jnp.int32))]
    return jnp.transpose(tab, (3, 0, 1, 2)), far


def _head_major_perm():
    hd = np.arange(HA * 2 * DA).reshape(HA, 2, DA)
    h, m, d = np.meshgrid(np.arange(HA), np.arange(2), np.arange(DA), indexing="ij")
    old = m * (HA * DA) + h * DA + d
    del hd
    perm_q = old.reshape(-1)
    return np.concatenate([perm_q, COL_AK + perm_q, np.arange(COL_AV, IN_COLS)])


def kernel(x, mem, pre_norm, post_norm, w_in, lambda_q1, lambda_k1, lambda_q2, lambda_k2,
           diff_subln, rel_bias, lb_logits, hgrn_norm, mem_norm, w_mem_kv, w_branch, w_out):
    b, l, d = x.shape
    t = b * l
    layer = 0
    att_t = min(512, l)

    lam_init = 0.8 - 0.6 * math.exp(-0.3 * layer)
    lam = (jnp.exp(jnp.sum(lambda_q1[layer].astype(F32) * lambda_k1[layer].astype(F32)))
           - jnp.exp(jnp.sum(lambda_q2[layer].astype(F32) * lambda_k2[layer].astype(F32)))
           + lam_init)
    bias_tab, far = _bias_tables(rel_bias, att_t)
    scalars = jnp.concatenate([lam.reshape(1), far.T.reshape(-1)]).astype(F32)
    lb_all = jnp.cumsum(jax.nn.softmax(lb_logits.astype(F32), axis=1), axis=1)
    lb2 = lb_all[:, layer, :]
    w_in_p = w_in[layer][:, _head_major_perm()].astype(BF16)
    blk = np.arange(HG_BLK)
    same = (blk[:, None] // CHUNK) == (blk[None, :] // CHUNK)
    tri_l = jnp.asarray(same & (blk[None, :] <= blk[:, None]), BF16)
    tri_u = jnp.asarray(same & (blk[None, :] >= blk[:, None]), BF16)

    proj2 = _norm_matmul(x.reshape(t, d), pre_norm[layer].reshape(1, d), w_in_p, tm=min(1024, t), tn=2048)
    proj3 = proj2.reshape(b, l, IN_COLS)

    ua = _diff_attention(proj3, scalars, bias_tab, diff_subln[layer].reshape(1, 2 * DA).astype(F32),
                         att_t, 1.0 - lam_init)

    ub = _hgrn(proj3, lb2, hgrn_norm[layer].reshape(1, VB).astype(F32), tri_l, tri_u)

    mkv = _norm_matmul(mem.reshape(b * N_MEM, d), mem_norm[layer].reshape(1, d),
                       w_mem_kv[layer].astype(BF16), tm=min(1024, b * N_MEM), tn=2048)
    uc = _mem_attention(proj3, mkv.reshape(b, N_MEM, 2 * HC * DC), tq=min(1024, l))

    out = _merge(x.reshape(t, d), ua.reshape(t, d), ub.reshape(t, d), uc.reshape(t, d), proj2,
                 w_branch[layer].astype(BF16), w_out[layer].astype(BF16),
                 post_norm[layer].reshape(1, d).astype(F32), tm=min(512, t))
    return out.reshape(b, l, d)
```

```python
import functools
import math

import jax
import jax.numpy as jnp
import numpy as np
from jax import lax
from jax.experimental import pallas as pl
from jax.experimental.pallas import tpu as pltpu

F32 = jnp.float32
BF16 = jnp.bfloat16

D_MODEL = 1024
N_MEM = 256
HA, DA = 8, 64
HB, KB, VB = 8, 128, 128
HC, DC = 4, 256
NUM_BUCKETS, MAX_DISTANCE = 32, 128
CHUNK = 64
EPS = 1e-6
LANE = 128

COL_AQ, COL_AK, COL_AV, COL_AZ = 0, 1024, 2048, 3072
COL_BQ, COL_BFF, COL_BFB, COL_BI, COL_BZ = 4096, 5120, 6144, 7168, 8192
COL_CQ, COL_CZ, COL_G = 9216, 10240, 11264
IN_COLS = 14336

VMEM_LIMIT = 56 * 1024 * 1024


def _cparams(n_axes):
    return pltpu.CompilerParams(dimension_semantics=("arbitrary",) * n_axes,
                                vmem_limit_bytes=VMEM_LIMIT)


def _norm_matmul_kernel(x_ref, nw_ref, w_ref, o_ref, h_ref):
    @pl.when(pl.program_id(1) == 0)
    def _():
        x = x_ref[...]
        ms = jnp.mean(x * x, axis=-1, keepdims=True)
        h_ref[...] = (x * lax.rsqrt(ms + EPS) * nw_ref[...]).astype(BF16)

    o_ref[...] = jnp.dot(h_ref[...], w_ref[...], preferred_element_type=F32).astype(o_ref.dtype)


def _norm_matmul(x2d, norm_w, w_bf16, tm, tn):
    m, d = x2d.shape
    n = w_bf16.shape[1]
    return pl.pallas_call(
        _norm_matmul_kernel,
        grid=(m // tm, n // tn),
        in_specs=[pl.BlockSpec((tm, d), lambda i, j: (i, 0)),
                  pl.BlockSpec((1, d), lambda i, j: (0, 0)),
                  pl.BlockSpec((d, tn), lambda i, j: (0, j))],
        out_specs=pl.BlockSpec((tm, tn), lambda i, j: (i, j)),
        out_shape=jax.ShapeDtypeStruct((m, n), BF16),
        scratch_shapes=[pltpu.VMEM((tm, d), BF16)],
        compiler_params=_cparams(2),
        name="norm_matmul",
    )(x2d, norm_w, w_bf16)


ATT_GROUP = 256
N_BIAS = 5


def _attn_kernel(sc_ref, q_ref, k_ref, v_ref, z_ref, bias_ref, sub_ref, o_ref,
                 vt_ref, qqt_ref, acc_ref, m_ref, l_ref, s_ref, p_ref, a_ref, *, t, nk, out_scale):
    i = pl.program_id(2)
    lam = sc_ref[0]
    ng = 2 * t // ATT_GROUP
    gpm = t // ATT_GROUP

    @pl.when(i == 0)
    def _():
        for c in range(nk):
            vt_ref[c] = v_ref[0, c * t:(c + 1) * t, :].astype(F32).T.astype(BF16)

    qt = (q_ref[0].astype(F32) * DA ** -0.5).T
    row = lax.broadcasted_iota(jnp.int32, (LANE, ATT_GROUP), 0)
    for g in range(ng):
        piece = qt[:, (g % gpm) * ATT_GROUP:(g % gpm + 1) * ATT_GROUP]
        keep = (row < DA) if g < gpm else (row >= DA)
        qqt_ref[g] = jnp.where(keep, piece, 0.0).astype(BF16)

    m_ref[...] = jnp.full(m_ref.shape, -jnp.inf, F32)
    l_ref[...] = jnp.zeros(l_ref.shape, F32)
    acc_ref[...] = jnp.zeros(acc_ref.shape, F32)

    def stage_a(j, g):
        k = k_ref[0, pl.ds(pl.multiple_of(j * t, t), t), :]
        s_ref[g % 2] = jnp.dot(k, qqt_ref[g], preferred_element_type=F32)

    def stage_b(j, g):
        idx = jnp.clip(j - i + 2, 0, N_BIAS - 1)
        s = s_ref[g % 2] + bias_ref[0, idx, g % gpm]
        m_prev = m_ref[g]
        m_new = jnp.maximum(m_prev, jnp.max(s, axis=0, keepdims=True))
        alpha = jnp.exp(m_prev - m_new)
        p = jnp.exp(s - m_new)
        l_ref[g] = alpha * l_ref[g] + jnp.sum(p, axis=0, keepdims=True)
        m_ref[g] = m_new
        a_ref[g % 2] = alpha
        p_ref[g % 2] = p.astype(BF16)

    def stage_c(j, g):
        acc_ref[g] = a_ref[g % 2] * acc_ref[g] + jnp.dot(vt_ref[j], p_ref[g % 2],
                                                         preferred_element_type=F32)

    def item_before(j, g, back):
        return (j, g - back) if g >= back else (j - 1, g - back + ng)

    def steps(j, first):
        for g in range(ng):
            if not (first and g < 2):
                stage_c(*item_before(j, g, 2))
            if not (first and g < 1):
                stage_b(*item_before(j, g, 1))
            stage_a(j, g)

    steps(0, True)

    def body(j, carry):
        steps(j, False)
        return carry

    lax.fori_loop(1, nk, body, 0)
    stage_c(nk - 1, ng - 2)
    stage_b(nk - 1, ng - 1)
    stage_c(nk - 1, ng - 1)

    d = jnp.concatenate(
        [acc_ref[g] / l_ref[g] - lam * (acc_ref[g + gpm] / l_ref[g + gpm]) for g in range(gpm)],
        axis=1).T
    ms = jnp.mean(d * d, axis=-1, keepdims=True)
    y = d * lax.rsqrt(ms + EPS) * sub_ref[...] * out_scale
    z = z_ref[0].astype(F32)
    o_ref[0] = (y * (z * jax.nn.sigmoid(z))).astype(o_ref.dtype)


def _diff_attention(proj3, scalars, bias_tab, subln, t, out_scale):
    b, l, _ = proj3.shape
    nk = l // t
    ng = 2 * t // ATT_GROUP
    cq, ck, cv, cz = COL_AQ // LANE, COL_AK // LANE, COL_AV // LANE, COL_AZ // LANE
    kern = functools.partial(_attn_kernel, t=t, nk=nk, out_scale=out_scale)
    return pl.pallas_call(
        kern,
        grid=(b, HA, nk),
        in_specs=[pl.BlockSpec(memory_space=pltpu.SMEM),
                  pl.BlockSpec((1, t, LANE), lambda b_, h, i: (b_, i, cq + h)),
                  pl.BlockSpec((1, l, LANE), lambda b_, h, i: (b_, 0, ck + h)),
                  pl.BlockSpec((1, l, LANE), lambda b_, h, i: (b_, 0, cv + h)),
                  pl.BlockSpec((1, t, LANE), lambda b_, h, i: (b_, i, cz + h)),
                  pl.BlockSpec((1, N_BIAS, t // ATT_GROUP, t, ATT_GROUP), lambda b_, h, i: (h, 0, 0, 0, 0)),
                  pl.BlockSpec((1, LANE), lambda b_, h, i: (0, 0))],
        out_specs=pl.BlockSpec((1, t, LANE), lambda b_, h, i: (b_, i, h)),
        out_shape=jax.ShapeDtypeStruct((b, l, HA * 2 * DA), BF16),
        scratch_shapes=[pltpu.VMEM((nk, LANE, t), BF16),
                        pltpu.VMEM((ng, LANE, ATT_GROUP), BF16),
                        pltpu.VMEM((ng, LANE, ATT_GROUP), F32),
                        pltpu.VMEM((ng, 1, ATT_GROUP), F32),
                        pltpu.VMEM((ng, 1, ATT_GROUP), F32),
                        pltpu.VMEM((2, t, ATT_GROUP), F32),
                        pltpu.VMEM((2, t, ATT_GROUP), BF16),
                        pltpu.VMEM((2, 1, ATT_GROUP), F32)],
        compiler_params=_cparams(3),
        name="diff_attention",
    )(scalars, proj3, proj3, proj3, proj3, bias_tab, subln)


HG_BLK = 256


def _hgrn_kernel(q_ref, ff_ref, fb_ref, v_ref, z_ref, lb_ref, nw_ref, tl_ref, tu_ref, o_ref,
                 qtf_ref, ktf_ref, khf_ref, qtb_ref, ktb_ref, khb_ref, qh_ref,
                 df_ref, db_ref, st_ref, sf_ref, sb_ref, *, l):
    n = l // CHUNK
    cpb = HG_BLK // CHUNK
    lb_f = lb_ref[0:1, :]
    lb_b = lb_ref[1:2, :]
    tl = tl_ref[...]
    tu = tu_ref[...]

    def cumsum(tri, g):
        g_hi = g.astype(BF16)
        g_lo = (g - g_hi.astype(F32)).astype(BF16)
        return (jnp.dot(tri, g_hi, preferred_element_type=F32)
                + jnp.dot(tri, g_lo, preferred_element_type=F32))

    def prep(r, carry):
        off = pl.multiple_of(r * HG_BLK, HG_BLK)
        rows = pl.ds(off, HG_BLK)
        q = q_ref[0, rows, :].astype(F32)

        def one_dir(f_ref, lb, tri, mid_row, last_row):
            x = f_ref[0, rows, :].astype(F32)
            g = jnp.log(lb + (1.0 - lb) * jax.nn.sigmoid(x))
            kk = (1.0 - lb) * jax.nn.sigmoid(-x)
            bcs = cumsum(tri, g).reshape(cpb, CHUNK, LANE)
            b_mid = bcs[:, mid_row:mid_row + 1, :]
            b_last = bcs[:, last_row:last_row + 1, :]
            q3 = q.reshape(cpb, CHUNK, LANE)
            k3 = kk.reshape(cpb, CHUNK, LANE)
            qt = (q3 * jnp.exp(bcs - b_mid)).reshape(HG_BLK, LANE)
            kt = (k3 * jnp.exp(b_mid - bcs)).reshape(HG_BLK, LANE)
            kh = (k3 * jnp.exp(b_last - bcs)).reshape(HG_BLK, LANE)
            qh = (q3 * jnp.exp(bcs)).reshape(HG_BLK, LANE)
            return qt, kt, kh, qh, jnp.exp(b_last)

        qt, kt, kh, qh_f, dec = one_dir(ff_ref, lb_f, tl, CHUNK // 2 - 1, CHUNK - 1)
        qtf_ref[rows, :] = qt.astype(BF16)
        ktf_ref[rows, :] = kt.astype(BF16)
        khf_ref[rows, :] = kh.astype(BF16)
        df_ref[pl.ds(r * cpb, cpb)] = dec
        qt, kt, kh, qh_b, dec = one_dir(fb_ref, lb_b, tu, CHUNK // 2, 0)
        qtb_ref[rows, :] = qt.astype(BF16)
        ktb_ref[rows, :] = kt.astype(BF16)
        khb_ref[rows, :] = kh.astype(BF16)
        db_ref[pl.ds(r * cpb, cpb)] = dec
        qh_ref[rows, :] = jnp.concatenate([qh_f, qh_b], axis=1).astype(BF16)
        return carry

    lax.fori_loop(0, l // HG_BLK, prep, 0)

    sf_ref[...] = jnp.zeros(sf_ref.shape, F32)
    sb_ref[...] = jnp.zeros(sb_ref.shape, F32)
    tn = (((0,), (0,)), ((), ()))

    def scan(t, carry):
        cf = pl.ds(pl.multiple_of(t * CHUNK, CHUNK), CHUNK)
        tb = n - 1 - t
        cb = pl.ds(pl.multiple_of(tb * CHUNK, CHUNK), CHUNK)
        s_f = sf_ref[...]
        s_b = sb_ref[...]
        st_ref[t, :, 0:KB] = s_f.astype(BF16)
        st_ref[tb, :, KB:2 * KB] = s_b.astype(BF16)
        u_f = lax.dot_general(v_ref[0, cf, :], khf_ref[cf, :], tn, preferred_element_type=F32)
        u_b = lax.dot_general(v_ref[0, cb, :], khb_ref[cb, :], tn, preferred_element_type=F32)
        sf_ref[...] = s_f * df_ref[t] + u_f
        sb_ref[...] = s_b * db_ref[tb] + u_b
        return carry

    lax.fori_loop(0, n, scan, 0)

    row = lax.broadcasted_iota(jnp.int32, (CHUNK, CHUNK), 0)
    col = lax.broadcasted_iota(jnp.int32, (CHUNK, CHUNK), 1)
    nt = (((1,), (1,)), ((), ()))
    nw = nw_ref[...]

    def out(c, carry):
        rows = pl.ds(pl.multiple_of(c * CHUNK, CHUNK), CHUNK)
        a_f = lax.dot_general(qtf_ref[rows, :], ktf_ref[rows, :], nt, preferred_element_type=F32)
        a_b = lax.dot_general(qtb_ref[rows, :], ktb_ref[rows, :], nt, preferred_element_type=F32)
        a = jnp.where(col <= row, a_f, 0.0) + jnp.where(col >= row, a_b, 0.0)
        o = (jnp.dot(a.astype(BF16), v_ref[0, rows, :], preferred_element_type=F32)
             + lax.dot_general(qh_ref[rows, :], st_ref[c], nt, preferred_element_type=F32))
        ms = jnp.mean(o * o, axis=-1, keepdims=True)
        y = o * lax.rsqrt(ms + EPS) * nw
        z = z_ref[0, rows, :].astype(F32)
        o_ref[0, rows, :] = (y * (z * jax.nn.sigmoid(z))).astype(o_ref.dtype)
        return carry

    lax.fori_loop(0, n, out, 0)


def _hgrn(proj3, lb2, hnorm, tri_l, tri_u):
    b, l, _ = proj3.shape
    n = l // CHUNK
    c = lambda col: col // LANE
    seq = lambda col: pl.BlockSpec((1, l, LANE), lambda b_, h: (b_, 0, c(col) + h))
    return pl.pallas_call(
        functools.partial(_hgrn_kernel, l=l),
        grid=(b, HB),
        in_specs=[seq(COL_BQ), seq(COL_BFF), seq(COL_BFB), seq(COL_BI), seq(COL_BZ),
                  pl.BlockSpec((2, LANE), lambda b_, h: (0, h)),
                  pl.BlockSpec((1, LANE), lambda b_, h: (0, 0)),
                  pl.BlockSpec((HG_BLK, HG_BLK), lambda b_, h: (0, 0)),
                  pl.BlockSpec((HG_BLK, HG_BLK), lambda b_, h: (0, 0))],
        out_specs=pl.BlockSpec((1, l, LANE), lambda b_, h: (b_, 0, h)),
        out_shape=jax.ShapeDtypeStruct((b, l, HB * VB), BF16),
        scratch_shapes=[pltpu.VMEM((l, KB), BF16)] * 6
                       + [pltpu.VMEM((l, 2 * KB), BF16),
                          pltpu.VMEM((n, 1, KB), F32), pltpu.VMEM((n, 1, KB), F32),
                          pltpu.VMEM((n, VB, 2 * KB), BF16),
                          pltpu.VMEM((VB, KB), F32), pltpu.VMEM((VB, KB), F32)],
        compiler_params=_cparams(2),
        name="hgrn2",
    )(proj3, proj3, proj3, proj3, proj3, lb2, hnorm, tri_l, tri_u)


def _mem_attn_kernel(q_ref, k_ref, v_ref, z_ref, o_ref):
    q = q_ref[0] * jnp.asarray(DC ** -0.5, BF16)
    s = lax.dot_general(q, k_ref[0], (((1,), (1,)), ((), ())), preferred_element_type=F32)
    m = jnp.max(s, axis=-1, keepdims=True)
    e = jnp.exp(s - m)
    p = e / jnp.sum(e, axis=-1, keepdims=True)
    o = jnp.dot(p.astype(BF16), v_ref[0], preferred_element_type=F32)
    z = z_ref[0].astype(F32)
    o_ref[0] = (o * (z * jax.nn.sigmoid(z))).astype(o_ref.dtype)


def _mem_attention(proj3, mkv3, tq):
    b, l, _ = proj3.shape
    cq, cz = COL_CQ // DC, COL_CZ // DC
    return pl.pallas_call(
        _mem_attn_kernel,
        grid=(b, HC, l // tq),
        in_specs=[pl.BlockSpec((1, tq, DC), lambda b_, h, i: (b_, i, cq + h)),
                  pl.BlockSpec((1, N_MEM, DC), lambda b_, h, i: (b_, 0, h)),
                  pl.BlockSpec((1, N_MEM, DC), lambda b_, h, i: (b_, 0, HC + h)),
                  pl.BlockSpec((1, tq, DC), lambda b_, h, i: (b_, i, cz + h))],
        out_specs=pl.BlockSpec((1, tq, DC), lambda b_, h, i: (b_, i, h)),
        out_shape=jax.ShapeDtypeStruct((b, l, HC * DC), BF16),
        compiler_params=_cparams(3),
        name="mem_attention",
    )(proj3, mkv3, mkv3, proj3)


def _merge_kernel(x_ref, ua_ref, ub_ref, uc_ref, ga_ref, gb_ref, gc_ref, wb_ref, wo_ref, pn_ref, o_ref):
    def branch(u_ref, g_ref, idx):
        y = jnp.dot(u_ref[...], wb_ref[idx], preferred_element_type=F32)
        return jax.nn.sigmoid(g_ref[...].astype(F32)) * y

    merged = branch(ua_ref, ga_ref, 0) + branch(ub_ref, gb_ref, 1) + branch(uc_ref, gc_ref, 2)
    y = jnp.dot(merged.astype(BF16), wo_ref[...], preferred_element_type=F32)
    ms = jnp.mean(y * y, axis=-1, keepdims=True)
    o_ref[...] = x_ref[...] + y * lax.rsqrt(ms + EPS) * pn_ref[...]


def _merge(x2d, ua, ub, uc, proj2, wb, wo, pn, tm):
    m, d = x2d.shape
    g0 = COL_G // d
    row = lambda: pl.BlockSpec((tm, d), lambda i: (i, 0))
    gate = lambda k: pl.BlockSpec((tm, d), lambda i: (i, g0 + k))
    return pl.pallas_call(
        _merge_kernel,
        grid=(m // tm,),
        in_specs=[row(), row(), row(), row(), gate(0), gate(1), gate(2),
                  pl.BlockSpec((3, d, d), lambda i: (0, 0, 0)),
                  pl.BlockSpec((d, d), lambda i: (0, 0)),
                  pl.BlockSpec((1, d), lambda i: (0, 0))],
        out_specs=row(),
        out_shape=jax.ShapeDtypeStruct((m, d), F32),
        compiler_params=_cparams(1),
        name="merge_out",
    )(x2d, ua, ub, uc, proj2, proj2, proj2, wb, wo, pn)


def _t5_bucket(rel):
    nb = NUM_BUCKETS // 2
    max_exact = nb // 2
    ret = jnp.where(rel > 0, nb, 0)
    n = jnp.abs(rel)
    nf = jnp.maximum(n, 1).astype(jnp.float32)
    large = max_exact + (jnp.log(nf / max_exact) / math.log(MAX_DISTANCE / max_exact)
                         * (nb - max_exact)).astype(jnp.int32)
    large = jnp.minimum(large, nb - 1)
    return ret + jnp.where(n < max_exact, n, large)


def _bias_tables(rel_bias, t):
    w = 2 * t
    k = jnp.arange(w, dtype=jnp.int32)
    k = jnp.where(k < t, k, k - w)
    d = t * (jnp.arange(N_BIAS, dtype=jnp.int32) - N_BIAS // 2)
    u = rel_bias.astype(F32)[_t5_bucket(d[:, None] - k[None, :])]
    u = jnp.transpose(u, (2, 0, 1))
    flat = jnp.tile(u, (1, 1, t))[:, :, :t * (w - 1)]
    tab = flat.reshape(HA, N_BIAS, t, w - 1)[:, :, :, :t]
    tab = tab.reshape(HA, N_BIAS, t, t // ATT_GROUP, ATT_GROUP)
    return jnp.transpose(tab, (0, 1, 3, 2, 4))


def _head_major(w):
    return w.reshape(D_MODEL, 2, HA, DA).transpose(0, 2, 1, 3).reshape(D_MODEL, 2 * HA * DA)


def kernel(x, mem, pre_norm, post_norm, w_in, lambda_q1, lambda_k1, lambda_q2, lambda_k2,
           diff_subln, rel_bias, lb_logits, hgrn_norm, mem_norm, w_mem_kv, w_branch, w_out):
    b, l, d = x.shape
    t = b * l
    layer = 0
    att_t = min(512, l)

    lam_init = 0.8 - 0.6 * math.exp(-0.3 * layer)
    lam = (jnp.exp(jnp.sum(lambda_q1[layer].astype(F32) * lambda_k1[layer].astype(F32)))
           - jnp.exp(jnp.sum(lambda_q2[layer].astype(F32) * lambda_k2[layer].astype(F32)))
           + lam_init)
    assert att_t >= MAX_DISTANCE and att_t % ATT_GROUP == 0
    bias_tab = _bias_tables(rel_bias, att_t)
    scalars = lam.reshape(1).astype(F32)
    lb_all = jnp.cumsum(jax.nn.softmax(lb_logits.astype(F32), axis=1), axis=1)
    lb2 = lb_all[:, layer, :]
    w_l = w_in[layer]
    w_in_p = jnp.concatenate([_head_major(w_l[:, COL_AQ:COL_AK]).astype(BF16),
                              _head_major(w_l[:, COL_AK:COL_AV]).astype(BF16),
                              w_l[:, COL_AV:].astype(BF16)], axis=1)
    blk = np.arange(HG_BLK)
    same = (blk[:, None] // CHUNK) == (blk[None, :] // CHUNK)
    tri_l = jnp.asarray(same & (blk[None, :] <= blk[:, None]), BF16)
    tri_u = jnp.asarray(same & (blk[None, :] >= blk[:, None]), BF16)

    proj2 = _norm_matmul(x.reshape(t, d), pre_norm[layer].reshape(1, d), w_in_p, tm=min(1024, t), tn=2048)
    proj3 = proj2.reshape(b, l, IN_COLS)

    ua = _diff_attention(proj3, scalars, bias_tab, diff_subln[layer].reshape(1, 2 * DA).astype(F32),
                         att_t, 1.0 - lam_init)

    ub = _hgrn(proj3, lb2, hgrn_norm[layer].reshape(1, VB).astype(F32), tri_l, tri_u)

    mkv = _norm_matmul(mem.reshape(b * N_MEM, d), mem_norm[layer].reshape(1, d),
                       w_mem_kv[layer].astype(BF16), tm=min(1024, b * N_MEM), tn=2048)
    uc = _mem_attention(proj3, mkv.reshape(b, N_MEM, 2 * HC * DC), tq=min(1024, l))

    out = _merge(x.reshape(t, d), ua.reshape(t, d), ub.reshape(t, d), uc.reshape(t, d), proj2,
                 w_branch[layer].astype(BF16), w_out[layer].astype(BF16),
                 post_norm[layer].reshape(1, d).astype(F32), tm=min(512, t))
    return out.reshape(b, l, d)
```

```python
import functools
import math

import jax
import jax.numpy as jnp
import numpy as np
from jax import lax
from jax.experimental import pallas as pl
from jax.experimental.pallas import tpu as pltpu

F32 = jnp.float32
BF16 = jnp.bfloat16

D_MODEL = 1024
N_MEM = 256
HA, DA = 8, 64
HB, KB, VB = 8, 128, 128
HC, DC = 4, 256
NUM_BUCKETS, MAX_DISTANCE = 32, 128
CHUNK = 64
EPS = 1e-6
LANE = 128

COL_AQ, COL_AK, COL_AV, COL_AZ = 0, 1024, 2048, 3072
COL_BQ, COL_BFF, COL_BFB, COL_BI, COL_BZ = 4096, 5120, 6144, 7168, 8192
COL_CQ, COL_CZ, COL_G = 9216, 10240, 11264
IN_COLS = 14336

VMEM_LIMIT = 56 * 1024 * 1024


def _cparams(n_axes):
    return pltpu.CompilerParams(dimension_semantics=("arbitrary",) * n_axes,
                                vmem_limit_bytes=VMEM_LIMIT)


def _norm_matmul_kernel(x_ref, nw_ref, w_ref, o_ref, h_ref):
    @pl.when(pl.program_id(1) == 0)
    def _():
        x = x_ref[...]
        ms = jnp.mean(x * x, axis=-1, keepdims=True)
        h_ref[...] = (x * lax.rsqrt(ms + EPS) * nw_ref[...]).astype(BF16)

    o_ref[...] = jnp.dot(h_ref[...], w_ref[...], preferred_element_type=F32).astype(o_ref.dtype)


def _norm_matmul(x2d, norm_w, w_bf16, tm, tn):
    m, d = x2d.shape
    n = w_bf16.shape[1]
    return pl.pallas_call(
        _norm_matmul_kernel,
        grid=(m // tm, n // tn),
        in_specs=[pl.BlockSpec((tm, d), lambda i, j: (i, 0)),
                  pl.BlockSpec((1, d), lambda i, j: (0, 0)),
                  pl.BlockSpec((d, tn), lambda i, j: (0, j))],
        out_specs=pl.BlockSpec((tm, tn), lambda i, j: (i, j)),
        out_shape=jax.ShapeDtypeStruct((m, n), BF16),
        scratch_shapes=[pltpu.VMEM((tm, d), BF16)],
        compiler_params=_cparams(2),
        name="norm_matmul",
    )(x2d, norm_w, w_bf16)


ATT_GROUP = 256
N_BIAS = 5


def _attn_kernel(sc_ref, q_ref, k_ref, v_ref, z_ref, bias_ref, sub_ref, o_ref,
                 vt_ref, qqt_ref, acc_ref, m_ref, l_ref, s_ref, p_ref, a_ref, *, t, nk, out_scale):
    i = pl.program_id(2)
    lam = sc_ref[0]
    ng = 2 * t // ATT_GROUP
    gpm = t // ATT_GROUP

    @pl.when(i == 0)
    def _():
        for c in range(nk):
            vt_ref[c] = v_ref[0, c * t:(c + 1) * t, :].astype(F32).T.astype(BF16)

    qt = (q_ref[0].astype(F32) * DA ** -0.5).T
    row = lax.broadcasted_iota(jnp.int32, (LANE, ATT_GROUP), 0)
    for g in range(ng):
        piece = qt[:, (g % gpm) * ATT_GROUP:(g % gpm + 1) * ATT_GROUP]
        keep = (row < DA) if g < gpm else (row >= DA)
        qqt_ref[g] = jnp.where(keep, piece, 0.0).astype(BF16)

    m_ref[...] = jnp.full(m_ref.shape, -jnp.inf, F32)
    l_ref[...] = jnp.zeros(l_ref.shape, F32)
    acc_ref[...] = jnp.zeros(acc_ref.shape, F32)

    def stage_a(j, g):
        k = k_ref[0, pl.ds(pl.multiple_of(j * t, t), t), :]
        s_ref[g % 2] = jnp.dot(k, qqt_ref[g], preferred_element_type=F32)

    def stage_b(j, g):
        idx = jnp.clip(j - i + 2, 0, N_BIAS - 1)
        s = s_ref[g % 2] + bias_ref[0, idx, g % gpm]
        m_prev = m_ref[g]
        m_new = jnp.maximum(m_prev, jnp.max(s, axis=0, keepdims=True))
        alpha = jnp.exp(m_prev - m_new)
        p = jnp.exp(s - m_new)
        l_ref[g] = alpha * l_ref[g] + jnp.sum(p, axis=0, keepdims=True)
        m_ref[g] = m_new
        a_ref[g % 2] = alpha
        p_ref[g % 2] = p.astype(BF16)

    def stage_c(j, g):
        acc_ref[g] = a_ref[g % 2] * acc_ref[g] + jnp.dot(vt_ref[j], p_ref[g % 2],
                                                         preferred_element_type=F32)

    def item_before(j, g, back):
        return (j, g - back) if g >= back else (j - 1, g - back + ng)

    def steps(j, first):
        for g in range(ng):
            if not (first and g < 2):
                stage_c(*item_before(j, g, 2))
            if not (first and g < 1):
                stage_b(*item_before(j, g, 1))
            stage_a(j, g)

    steps(0, True)

    def body(j, carry):
        steps(j, False)
        return carry

    lax.fori_loop(1, nk, body, 0)
    stage_c(nk - 1, ng - 2)
    stage_b(nk - 1, ng - 1)
    stage_c(nk - 1, ng - 1)

    d = jnp.concatenate(
        [acc_ref[g] / l_ref[g] - lam * (acc_ref[g + gpm] / l_ref[g + gpm]) for g in range(gpm)],
        axis=1).T
    ms = jnp.mean(d * d, axis=-1, keepdims=True)
    y = d * lax.rsqrt(ms + EPS) * sub_ref[...] * out_scale
    z = z_ref[0].astype(F32)
    o_ref[0] = (y * (z * jax.nn.sigmoid(z))).astype(o_ref.dtype)


def _diff_attention(proj3, scalars, bias_tab, subln, t, out_scale):
    b, l, _ = proj3.shape
    nk = l // t
    ng = 2 * t // ATT_GROUP
    cq, ck, cv, cz = COL_AQ // LANE, COL_AK // LANE, COL_AV // LANE, COL_AZ // LANE
    kern = functools.partial(_attn_kernel, t=t, nk=nk, out_scale=out_scale)
    return pl.pallas_call(
        kern,
        grid=(b, HA, nk),
        in_specs=[pl.BlockSpec(memory_space=pltpu.SMEM),
                  pl.BlockSpec((1, t, LANE), lambda b_, h, i: (b_, i, cq + h)),
                  pl.BlockSpec((1, l, LANE), lambda b_, h, i: (b_, 0, ck + h)),
                  pl.BlockSpec((1, l, LANE), lambda b_, h, i: (b_, 0, cv + h)),
                  pl.BlockSpec((1, t, LANE), lambda b_, h, i: (b_, i, cz + h)),
                  pl.BlockSpec((1, N_BIAS, t // ATT_GROUP, t, ATT_GROUP), lambda b_, h, i: (h, 0, 0, 0, 0)),
                  pl.BlockSpec((1, LANE), lambda b_, h, i: (0, 0))],
        out_specs=pl.BlockSpec((1, t, LANE), lambda b_, h, i: (b_, i, h)),
        out_shape=jax.ShapeDtypeStruct((b, l, HA * 2 * DA), BF16),
        scratch_shapes=[pltpu.VMEM((nk, LANE, t), BF16),
                        pltpu.VMEM((ng, LANE, ATT_GROUP), BF16),
                        pltpu.VMEM((ng, LANE, ATT_GROUP), F32),
                        pltpu.VMEM((ng, 1, ATT_GROUP), F32),
                        pltpu.VMEM((ng, 1, ATT_GROUP), F32),
                        pltpu.VMEM((2, t, ATT_GROUP), F32),
                        pltpu.VMEM((2, t, ATT_GROUP), BF16),
                        pltpu.VMEM((2, 1, ATT_GROUP), F32)],
        compiler_params=_cparams(3),
        name="diff_attention",
    )(scalars, proj3, proj3, proj3, proj3, bias_tab, subln)


HG_BLK = 256


def _hgrn_kernel(q_ref, ff_ref, fb_ref, v_ref, z_ref, lb_ref, nw_ref, tl_ref, tu_ref, ml_ref, mu_ref, o_ref,
                 qtf_ref, ktf_ref, qtb_ref, ktb_ref, qh_ref, st_ref, sf_ref, sb_ref,
                 kh_ref, dec_ref, a_ref, *, l):
    nblk = l // HG_BLK
    cpb = HG_BLK // CHUNK
    tn = (((0,), (0,)), ((), ()))
    nt = (((1,), (1,)), ((), ()))
    dirs = ((ff_ref, 0, tl_ref, CHUNK // 2 - 1, CHUNK - 1, qtf_ref, ktf_ref, sf_ref),
            (fb_ref, 1, tu_ref, CHUNK // 2, 0, qtb_ref, ktb_ref, sb_ref))

    def block_of(r, di):
        return r if di == 0 else nblk - 1 - r

    def cumsum(tri, g):
        g_hi = g.astype(BF16)
        g_lo = (g - g_hi.astype(F32)).astype(BF16)
        return (jnp.dot(tri, g_hi, preferred_element_type=F32)
                + jnp.dot(tri, g_lo, preferred_element_type=F32))

    def prep(r):
        for f_ref, di, tri_ref, mid_row, last_row, qt_ref, kt_ref, _ in dirs:
            rows = pl.ds(pl.multiple_of(block_of(r, di) * HG_BLK, HG_BLK), HG_BLK)
            lb = lb_ref[di:di + 1, :]
            x = f_ref[0, rows, :].astype(F32)
            q3 = q_ref[0, rows, :].astype(F32).reshape(cpb, CHUNK, LANE)
            f = lb + (1.0 - lb) * jax.nn.sigmoid(x)
            g = jnp.log(f)
            k3 = (1.0 - f).reshape(cpb, CHUNK, LANE)
            bcs = cumsum(tri_ref[...], g).reshape(cpb, CHUNK, LANE)
            b_mid = bcs[:, mid_row:mid_row + 1, :]
            b_last = bcs[:, last_row:last_row + 1, :]
            qt = q3 * jnp.exp(bcs - b_mid)
            kt = k3 * jnp.exp(b_mid - bcs)
            qh = qt * jnp.exp(b_mid)
            kh = kt * jnp.exp(b_last - b_mid)
            qt_ref[rows, :] = qt.reshape(HG_BLK, LANE).astype(BF16)
            kt_ref[rows, :] = kt.reshape(HG_BLK, LANE).astype(BF16)
            qh_ref[rows, di * KB:(di + 1) * KB] = qh.reshape(HG_BLK, LANE).astype(BF16)
            kh_ref[r % 2, di] = kh.reshape(HG_BLK, LANE).astype(BF16)
            dec_ref[r % 2, di] = jnp.exp(b_last)

    def scan(r):
        for _, di, _, _, _, _, _, s_ref in dirs:
            blk = block_of(r, di)
            s = s_ref[...]
            for c in (range(cpb) if di == 0 else reversed(range(cpb))):
                st_ref[blk * cpb + c, :, di * KB:(di + 1) * KB] = s.astype(BF16)
                vrows = pl.ds(pl.multiple_of(blk * HG_BLK + c * CHUNK, CHUNK), CHUNK)
                u = lax.dot_general(v_ref[0, vrows, :], kh_ref[r % 2, di, c * CHUNK:(c + 1) * CHUNK, :], tn,
                                    preferred_element_type=F32)
                s = s * dec_ref[r % 2, di, c] + u
            s_ref[...] = s

    sf_ref[...] = jnp.zeros(sf_ref.shape, F32)
    sb_ref[...] = jnp.zeros(sb_ref.shape, F32)
    prep(0)

    def pass_a(r, carry):
        scan(r - 1)
        prep(r)
        return carry

    lax.fori_loop(1, nblk, pass_a, 0)
    scan(nblk - 1)

    nw = nw_ref[...]

    def scores(r):
        rows = pl.ds(pl.multiple_of(r * HG_BLK, HG_BLK), HG_BLK)
        a_f = lax.dot_general(qtf_ref[rows, :], ktf_ref[rows, :], nt, preferred_element_type=F32)
        a_b = lax.dot_general(qtb_ref[rows, :], ktb_ref[rows, :], nt, preferred_element_type=F32)
        a = jnp.where(ml_ref[...] != 0.0, a_f, 0.0) + jnp.where(mu_ref[...] != 0.0, a_b, 0.0)
        a_ref[r % 2] = a.astype(BF16)

    def outputs(r):
        rows = pl.ds(pl.multiple_of(r * HG_BLK, HG_BLK), HG_BLK)
        inter = jnp.concatenate(
            [lax.dot_general(qh_ref[pl.ds(pl.multiple_of(r * HG_BLK + c * CHUNK, CHUNK), CHUNK), :],
                             st_ref[r * cpb + c], nt, preferred_element_type=F32) for c in range(cpb)], axis=0)
        o = jnp.dot(a_ref[r % 2], v_ref[0, rows, :], preferred_element_type=F32) + inter
        ms = jnp.mean(o * o, axis=-1, keepdims=True)
        y = o * lax.rsqrt(ms + EPS) * nw
        z = z_ref[0, rows, :].astype(F32)
        o_ref[0, rows, :] = (y * (z * jax.nn.sigmoid(z))).astype(o_ref.dtype)

    scores(0)

    def pass_b(r, carry):
        outputs(r - 1)
        scores(r)
        return carry

    lax.fori_loop(1, nblk, pass_b, 0)
    outputs(nblk - 1)


def _hgrn(proj3, lb2, hnorm, tri_l, tri_u):
    b, l, _ = proj3.shape
    n = l // CHUNK
    cpb = HG_BLK // CHUNK
    c = lambda col: col // LANE
    seq = lambda col: pl.BlockSpec((1, l, LANE), lambda b_, h: (b_, 0, c(col) + h))
    blk = lambda: pl.BlockSpec((HG_BLK, HG_BLK), lambda b_, h: (0, 0))
    return pl.pallas_call(
        functools.partial(_hgrn_kernel, l=l),
        grid=(b, HB),
        in_specs=[seq(COL_BQ), seq(COL_BFF), seq(COL_BFB), seq(COL_BI), seq(COL_BZ),
                  pl.BlockSpec((2, LANE), lambda b_, h: (0, h)),
                  pl.BlockSpec((1, LANE), lambda b_, h: (0, 0)),
                  blk(), blk(), blk(), blk()],
        out_specs=pl.BlockSpec((1, l, LANE), lambda b_, h: (b_, 0, h)),
        out_shape=jax.ShapeDtypeStruct((b, l, HB * VB), BF16),
        scratch_shapes=[pltpu.VMEM((l, KB), BF16)] * 4
                       + [pltpu.VMEM((l, 2 * KB), BF16),
                          pltpu.VMEM((n, VB, 2 * KB), BF16),
                          pltpu.VMEM((VB, KB), F32), pltpu.VMEM((VB, KB), F32),
                          pltpu.VMEM((2, 2, HG_BLK, KB), BF16),
                          pltpu.VMEM((2, 2, cpb, 1, KB), F32),
                          pltpu.VMEM((2, HG_BLK, HG_BLK), BF16)],
        compiler_params=_cparams(2),
        name="hgrn2",
    )(proj3, proj3, proj3, proj3, proj3, lb2, hnorm, tri_l, tri_u,
      tri_l.astype(F32), tri_u.astype(F32))


def _mem_attn_kernel(q_ref, k_ref, v_ref, z_ref, o_ref):
    q = q_ref[0] * jnp.asarray(DC ** -0.5, BF16)
    s = lax.dot_general(q, k_ref[0], (((1,), (1,)), ((), ())), preferred_element_type=F32)
    m = jnp.max(s, axis=-1, keepdims=True)
    e = jnp.exp(s - m)
    p = e / jnp.sum(e, axis=-1, keepdims=True)
    o = jnp.dot(p.astype(BF16), v_ref[0], preferred_element_type=F32)
    z = z_ref[0].astype(F32)
    o_ref[0] = (o * (z * jax.nn.sigmoid(z))).astype(o_ref.dtype)


def _mem_attention(proj3, mkv3, tq):
    b, l, _ = proj3.shape
    cq, cz = COL_CQ // DC, COL_CZ // DC
    return pl.pallas_call(
        _mem_attn_kernel,
        grid=(b, HC, l // tq),
        in_specs=[pl.BlockSpec((1, tq, DC), lambda b_, h, i: (b_, i, cq + h)),
                  pl.BlockSpec((1, N_MEM, DC), lambda b_, h, i: (b_, 0, h)),
                  pl.BlockSpec((1, N_MEM, DC), lambda b_, h, i: (b_, 0, HC + h)),
                  pl.BlockSpec((1, tq, DC), lambda b_, h, i: (b_, i, cz + h))],
        out_specs=pl.BlockSpec((1, tq, DC), lambda b_, h, i: (b_, i, h)),
        out_shape=jax.ShapeDtypeStruct((b, l, HC * DC), BF16),
        compiler_params=_cparams(3),
        name="mem_attention",
    )(proj3, mkv3, mkv3, proj3)


def _merge_kernel(x_ref, ua_ref, ub_ref, uc_ref, ga_ref, gb_ref, gc_ref, wb_ref, wo_ref, pn_ref, o_ref):
    def branch(u_ref, g_ref, idx):
        y = jnp.dot(u_ref[...], wb_ref[idx], preferred_element_type=F32)
        return jax.nn.sigmoid(g_ref[...].astype(F32)) * y

    merged = branch(ua_ref, ga_ref, 0) + branch(ub_ref, gb_ref, 1) + branch(uc_ref, gc_ref, 2)
    y = jnp.dot(merged.astype(BF16), wo_ref[...], preferred_element_type=F32)
    ms = jnp.mean(y * y, axis=-1, keepdims=True)
    o_ref[...] = x_ref[...] + y * lax.rsqrt(ms + EPS) * pn_ref[...]


def _merge(x2d, ua, ub, uc, proj2, wb, wo, pn, tm):
    m, d = x2d.shape
    g0 = COL_G // d
    row = lambda: pl.BlockSpec((tm, d), lambda i: (i, 0))
    gate = lambda k: pl.BlockSpec((tm, d), lambda i: (i, g0 + k))
    return pl.pallas_call(
        _merge_kernel,
        grid=(m // tm,),
        in_specs=[row(), row(), row(), row(), gate(0), gate(1), gate(2),
                  pl.BlockSpec((3, d, d), lambda i: (0, 0, 0)),
                  pl.BlockSpec((d, d), lambda i: (0, 0)),
                  pl.BlockSpec((1, d), lambda i: (0, 0))],
        out_specs=row(),
        out_shape=jax.ShapeDtypeStruct((m, d), F32),
        compiler_params=_cparams(1),
        name="merge_out",
    )(x2d, ua, ub, uc, proj2, proj2, proj2, wb, wo, pn)


def _t5_bucket(rel):
    nb = NUM_BUCKETS // 2
    max_exact = nb // 2
    ret = jnp.where(rel > 0, nb, 0)
    n = jnp.abs(rel)
    nf = jnp.maximum(n, 1).astype(jnp.float32)
    large = max_exact + (jnp.log(nf / max_exact) / math.log(MAX_DISTANCE / max_exact)
                         * (nb - max_exact)).astype(jnp.int32)
    large = jnp.minimum(large, nb - 1)
    return ret + jnp.where(n < max_exact, n, large)


def _bias_tables(rel_bias, t):
    gpm = t // ATT_GROUP
    rel_1d = jnp.arange(-2 * t + 1, 2 * t, dtype=jnp.int32)
    by_rel = rel_bias.astype(F32)[_t5_bucket(rel_1d)].T
    key = np.arange(t)[None, None, :, None]
    qry = (np.arange(gpm)[None, :, None, None] * ATT_GROUP + np.arange(ATT_GROUP)[None, None, None, :])
    d = (np.arange(-1, 2) * t)[:, None, None, None]
    near = by_rel[:, d + key - qry + (2 * t - 1)]
    far = rel_bias.astype(F32)[_t5_bucket(jnp.asarray([-MAX_DISTANCE, MAX_DISTANCE], jnp.int32))].T
    far = jnp.broadcast_to(far[:, :, None, None, None], (HA, 2, gpm, t, ATT_GROUP))
    return jnp.concatenate([far[:, 0:1], near, far[:, 1:2]], axis=1)


def _head_major(w):
    return w.reshape(D_MODEL, 2, HA, DA).transpose(0, 2, 1, 3).reshape(D_MODEL, 2 * HA * DA)


def kernel(x, mem, pre_norm, post_norm, w_in, lambda_q1, lambda_k1, lambda_q2, lambda_k2,
           diff_subln, rel_bias, lb_logits, hgrn_norm, mem_norm, w_mem_kv, w_branch, w_out):
    b, l, d = x.shape
    t = b * l
    layer = 0
    att_t = min(512, l)

    lam_init = 0.8 - 0.6 * math.exp(-0.3 * layer)
    lam = (jnp.exp(jnp.sum(lambda_q1[layer].astype(F32) * lambda_k1[layer].astype(F32)))
           - jnp.exp(jnp.sum(lambda_q2[layer].astype(F32) * lambda_k2[layer].astype(F32)))
           + lam_init)
    assert att_t >= MAX_DISTANCE and att_t % ATT_GROUP == 0
    bias_tab = _bias_tables(rel_bias, att_t)
    scalars = lam.reshape(1).astype(F32)
    lb_all = jnp.cumsum(jax.nn.softmax(lb_logits.astype(F32), axis=1), axis=1)
    lb2 = lb_all[:, layer, :]
    w_l = w_in[layer]
    w_in_p = jnp.concatenate([_head_major(w_l[:, COL_AQ:COL_AK]).astype(BF16),
                              _head_major(w_l[:, COL_AK:COL_AV]).astype(BF16),
                              w_l[:, COL_AV:].astype(BF16)], axis=1)
    blk = np.arange(HG_BLK)
    same = (blk[:, None] // CHUNK) == (blk[None, :] // CHUNK)
    tri_l = jnp.asarray(same & (blk[None, :] <= blk[:, None]), BF16)
    tri_u = jnp.asarray(same & (blk[None, :] >= blk[:, None]), BF16)

    proj2 = _norm_matmul(x.reshape(t, d), pre_norm[layer].reshape(1, d), w_in_p, tm=min(1024, t), tn=2048)
    proj3 = proj2.reshape(b, l, IN_COLS)

    ua = _diff_attention(proj3, scalars, bias_tab, diff_subln[layer].reshape(1, 2 * DA).astype(F32),
                         att_t, 1.0 - lam_init)

    ub = _hgrn(proj3, lb2, hgrn_norm[layer].reshape(1, VB).astype(F32), tri_l, tri_u)

    mkv = _norm_matmul(mem.reshape(b * N_MEM, d), mem_norm[layer].reshape(1, d),
                       w_mem_kv[layer].astype(BF16), tm=min(1024, b * N_MEM), tn=2048)
    uc = _mem_attention(proj3, mkv.reshape(b, N_MEM, 2 * HC * DC), tq=min(1024, l))

    out = _merge(x.reshape(t, d), ua.reshape(t, d), ub.reshape(t, d), uc.reshape(t, d), proj2,
                 w_branch[layer].astype(BF16), w_out[layer].astype(BF16),
                 post_norm[layer].reshape(1, d).astype(F32), tm=min(512, t))
    return out.reshape(b, l, d)
```

```python
import functools
import math

import jax
import jax.numpy as jnp
import numpy as np
from jax import lax
from jax.experimental import pallas as pl
from jax.experimental.pallas import tpu as pltpu

F32 = jnp.float32
BF16 = jnp.bfloat16

D_MODEL = 1024
N_MEM = 256
HA, DA = 8, 64
HB, KB, VB = 8, 128, 128
HC, DC = 4, 256
NUM_BUCKETS, MAX_DISTANCE = 32, 128
CHUNK = 64
EPS = 1e-6
LANE = 128

COL_AQ, COL_AK, COL_AV, COL_AZ = 0, 1024, 2048, 3072
COL_BQ, COL_BFF, COL_BFB, COL_BI, COL_BZ = 4096, 5120, 6144, 7168, 8192
COL_CQ, COL_CZ, COL_G = 9216, 10240, 11264
IN_COLS = 14336

VMEM_LIMIT = 56 * 1024 * 1024


def _cparams(n_axes):
    return pltpu.CompilerParams(dimension_semantics=("arbitrary",) * n_axes,
                                vmem_limit_bytes=VMEM_LIMIT)


def _norm_matmul_kernel(x_ref, nw_ref, w_ref, o_ref, h_ref):
    @pl.when(pl.program_id(1) == 0)
    def _():
        x = x_ref[...]
        ms = jnp.mean(x * x, axis=-1, keepdims=True)
        h_ref[...] = (x * lax.rsqrt(ms + EPS) * nw_ref[...]).astype(BF16)

    o_ref[...] = jnp.dot(h_ref[...], w_ref[...], preferred_element_type=F32).astype(o_ref.dtype)


def _norm_matmul(x2d, norm_w, w_bf16, tm, tn):
    m, d = x2d.shape
    n = w_bf16.shape[1]
    return pl.pallas_call(
        _norm_matmul_kernel,
        grid=(m // tm, n // tn),
        in_specs=[pl.BlockSpec((tm, d), lambda i, j: (i, 0)),
                  pl.BlockSpec((1, d), lambda i, j: (0, 0)),
                  pl.BlockSpec((d, tn), lambda i, j: (0, j))],
        out_specs=pl.BlockSpec((tm, tn), lambda i, j: (i, j)),
        out_shape=jax.ShapeDtypeStruct((m, n), BF16),
        scratch_shapes=[pltpu.VMEM((tm, d), BF16)],
        compiler_params=_cparams(2),
        name="norm_matmul",
    )(x2d, norm_w, w_bf16)


ATT_GROUP = 256
ATT_TQ, ATT_TK = 1024, 512
BIAS_BLK = 128
N_BIAS = 5
LOG2E = math.log2(math.e)


def _attn_kernel(sc_ref, q_ref, k_ref, v_ref, z_ref, bias_ref, sub_ref, o_ref,
                 vt_ref, qqt_ref, acc_ref, m_ref, l_ref, s_ref, mx_ref, p_ref, a_ref,
                 *, tq, tk, nk, out_scale):
    i = pl.program_id(2)
    lam = sc_ref[0]
    ng = 2 * tq // ATT_GROUP
    gpm = tq // ATT_GROUP
    kblks, qblks = tk // BIAS_BLK, ATT_GROUP // BIAS_BLK

    @pl.when(i == 0)
    def _():
        for c in range(nk):
            vt_ref[c] = v_ref[0, c * tk:(c + 1) * tk, :].astype(F32).T.astype(BF16)

    qt = (q_ref[0].astype(F32) * (DA ** -0.5 * LOG2E)).T
    row = lax.broadcasted_iota(jnp.int32, (LANE, ATT_GROUP), 0)
    for g in range(ng):
        piece = qt[:, (g % gpm) * ATT_GROUP:(g % gpm + 1) * ATT_GROUP]
        keep = (row < DA) if g < gpm else (row >= DA)
        qqt_ref[g] = jnp.where(keep, piece, 0.0).astype(BF16)

    m_ref[...] = jnp.full(m_ref.shape, -jnp.inf, F32)
    l_ref[...] = jnp.zeros(l_ref.shape, F32)
    acc_ref[...] = jnp.zeros(acc_ref.shape, F32)

    def bias_slab(j, g):
        first = j * kblks - (i * (tq // BIAS_BLK) + (g % gpm) * qblks)
        return jnp.concatenate(
            [jnp.concatenate(
                [bias_ref[0, jnp.clip(first + kb - qb, -(N_BIAS // 2), N_BIAS // 2) + N_BIAS // 2]
                 for qb in range(qblks)], axis=1)
             for kb in range(kblks)], axis=0)

    def stage_a(j, g):
        k = k_ref[0, pl.ds(pl.multiple_of(j * tk, tk), tk), :]
        s = jnp.dot(k, qqt_ref[g], preferred_element_type=F32) + bias_slab(j, g)
        s_ref[g % 2] = s
        mx_ref[g % 2] = jnp.max(s, axis=0, keepdims=True)

    def stage_b(j, g):
        m_prev = m_ref[g]
        m_new = jnp.maximum(m_prev, mx_ref[g % 2])
        alpha = jnp.exp2(m_prev - m_new)
        p = jnp.exp2(s_ref[g % 2] - m_new)
        l_ref[g] = alpha * l_ref[g] + jnp.sum(p, axis=0, keepdims=True)
        m_ref[g] = m_new
        a_ref[g % 2] = alpha
        p_ref[g % 2] = p.astype(BF16)

    def stage_c(j, g):
        acc_ref[g] = a_ref[g % 2] * acc_ref[g] + jnp.dot(vt_ref[j], p_ref[g % 2],
                                                         preferred_element_type=F32)

    def item_before(j, g, back):
        return (j, g - back) if g >= back else (j - 1, g - back + ng)

    def steps(j, first):
        for g in range(ng):
            if not (first and g < 2):
                stage_c(*item_before(j, g, 2))
            if not (first and g < 1):
                stage_b(*item_before(j, g, 1))
            stage_a(j, g)

    steps(0, True)

    def body(j, carry):
        steps(j, False)
        return carry

    lax.fori_loop(1, nk, body, 0)
    stage_c(nk - 1, ng - 2)
    stage_b(nk - 1, ng - 1)
    stage_c(nk - 1, ng - 1)

    d = jnp.concatenate(
        [acc_ref[g] / l_ref[g] - lam * (acc_ref[g + gpm] / l_ref[g + gpm]) for g in range(gpm)],
        axis=1).T
    ms = jnp.mean(d * d, axis=-1, keepdims=True)
    y = d * lax.rsqrt(ms + EPS) * sub_ref[...] * out_scale
    z = z_ref[0].astype(F32)
    o_ref[0] = (y * (z * jax.nn.sigmoid(z))).astype(o_ref.dtype)


def _diff_attention(proj3, scalars, bias_blk, subln, tq, tk, out_scale):
    b, l, _ = proj3.shape
    nk = l // tk
    ng = 2 * tq // ATT_GROUP
    cq, ck, cv, cz = COL_AQ // LANE, COL_AK // LANE, COL_AV // LANE, COL_AZ // LANE
    kern = functools.partial(_attn_kernel, tq=tq, tk=tk, nk=nk, out_scale=out_scale)
    return pl.pallas_call(
        kern,
        grid=(b, HA, l // tq),
        in_specs=[pl.BlockSpec(memory_space=pltpu.SMEM),
                  pl.BlockSpec((1, tq, LANE), lambda b_, h, i: (b_, i, cq + h)),
                  pl.BlockSpec((1, l, LANE), lambda b_, h, i: (b_, 0, ck + h)),
                  pl.BlockSpec((1, l, LANE), lambda b_, h, i: (b_, 0, cv + h)),
                  pl.BlockSpec((1, tq, LANE), lambda b_, h, i: (b_, i, cz + h)),
                  pl.BlockSpec((1, N_BIAS, BIAS_BLK, BIAS_BLK), lambda b_, h, i: (h, 0, 0, 0)),
                  pl.BlockSpec((1, LANE), lambda b_, h, i: (0, 0))],
        out_specs=pl.BlockSpec((1, tq, LANE), lambda b_, h, i: (b_, i, h)),
        out_shape=jax.ShapeDtypeStruct((b, l, HA * 2 * DA), BF16),
        scratch_shapes=[pltpu.VMEM((nk, LANE, tk), BF16),
                        pltpu.VMEM((ng, LANE, ATT_GROUP), BF16),
                        pltpu.VMEM((ng, LANE, ATT_GROUP), F32),
                        pltpu.VMEM((ng, 1, ATT_GROUP), F32),
                        pltpu.VMEM((ng, 1, ATT_GROUP), F32),
                        pltpu.VMEM((2, tk, ATT_GROUP), F32),
                        pltpu.VMEM((2, 1, ATT_GROUP), F32),
                        pltpu.VMEM((2, tk, ATT_GROUP), BF16),
                        pltpu.VMEM((2, 1, ATT_GROUP), F32)],
        compiler_params=_cparams(3),
        name="diff_attention",
    )(scalars, proj3, proj3, proj3, proj3, bias_blk, subln)


HG_BLK = 256


def _hgrn_kernel(q_ref, ff_ref, fb_ref, v_ref, z_ref, lb_ref, nw_ref, tl_ref, tu_ref, ml_ref, mu_ref, o_ref,
                 qtf_ref, ktf_ref, qtb_ref, ktb_ref, qh_ref, st_ref, sf_ref, sb_ref,
                 kh_ref, dec_ref, a_ref, *, l):
    nblk = l // HG_BLK
    cpb = HG_BLK // CHUNK
    tn = (((0,), (0,)), ((), ()))
    nt = (((1,), (1,)), ((), ()))
    dirs = ((ff_ref, 0, tl_ref, CHUNK // 2 - 1, CHUNK - 1, qtf_ref, ktf_ref, sf_ref),
            (fb_ref, 1, tu_ref, CHUNK // 2, 0, qtb_ref, ktb_ref, sb_ref))

    def block_of(r, di):
        return r if di == 0 else nblk - 1 - r

    def cumsum(tri, g):
        g_hi = g.astype(BF16)
        g_lo = (g - g_hi.astype(F32)).astype(BF16)
        return (jnp.dot(tri, g_hi, preferred_element_type=F32)
                + jnp.dot(tri, g_lo, preferred_element_type=F32))

    def prep(r):
        for f_ref, di, tri_ref, mid_row, last_row, qt_ref, kt_ref, _ in dirs:
            rows = pl.ds(pl.multiple_of(block_of(r, di) * HG_BLK, HG_BLK), HG_BLK)
            lb = lb_ref[di:di + 1, :]
            x = f_ref[0, rows, :].astype(F32)
            q3 = q_ref[0, rows, :].astype(F32).reshape(cpb, CHUNK, LANE)
            f = lb + (1.0 - lb) * jax.nn.sigmoid(x)
            g = jnp.log(f)
            k3 = (1.0 - f).reshape(cpb, CHUNK, LANE)
            bcs = cumsum(tri_ref[...], g).reshape(cpb, CHUNK, LANE)
            b_mid = bcs[:, mid_row:mid_row + 1, :]
            b_last = bcs[:, last_row:last_row + 1, :]
            qt = q3 * jnp.exp(bcs - b_mid)
            kt = k3 * jnp.exp(b_mid - bcs)
            qh = qt * jnp.exp(b_mid)
            kh = kt * jnp.exp(b_last - b_mid)
            qt_ref[rows, :] = qt.reshape(HG_BLK, LANE).astype(BF16)
            kt_ref[rows, :] = kt.reshape(HG_BLK, LANE).astype(BF16)
            qh_ref[rows, di * KB:(di + 1) * KB] = qh.reshape(HG_BLK, LANE).astype(BF16)
            kh_ref[r % 2, di] = kh.reshape(HG_BLK, LANE).astype(BF16)
            dec_ref[r % 2, di] = jnp.exp(b_last)

    def scan(r):
        for _, di, _, _, _, _, _, s_ref in dirs:
            blk = block_of(r, di)
            s = s_ref[...]
            for c in (range(cpb) if di == 0 else reversed(range(cpb))):
                st_ref[blk * cpb + c, :, di * KB:(di + 1) * KB] = s.astype(BF16)
                vrows = pl.ds(pl.multiple_of(blk * HG_BLK + c * CHUNK, CHUNK), CHUNK)
                u = lax.dot_general(v_ref[0, vrows, :], kh_ref[r % 2, di, c * CHUNK:(c + 1) * CHUNK, :], tn,
                                    preferred_element_type=F32)
                s = s * dec_ref[r % 2, di, c] + u
            s_ref[...] = s

    sf_ref[...] = jnp.zeros(sf_ref.shape, F32)
    sb_ref[...] = jnp.zeros(sb_ref.shape, F32)
    prep(0)

    def pass_a(r, carry):
        scan(r - 1)
        prep(r)
        return carry

    lax.fori_loop(1, nblk, pass_a, 0)
    scan(nblk - 1)

    nw = nw_ref[...]

    def scores(r):
        rows = pl.ds(pl.multiple_of(r * HG_BLK, HG_BLK), HG_BLK)
        a_f = lax.dot_general(qtf_ref[rows, :], ktf_ref[rows, :], nt, preferred_element_type=F32)
        a_b = lax.dot_general(qtb_ref[rows, :], ktb_ref[rows, :], nt, preferred_element_type=F32)
        a = jnp.where(ml_ref[...] != 0.0, a_f, 0.0) + jnp.where(mu_ref[...] != 0.0, a_b, 0.0)
        a_ref[r % 2] = a.astype(BF16)

    def outputs(r):
        rows = pl.ds(pl.multiple_of(r * HG_BLK, HG_BLK), HG_BLK)
        inter = jnp.concatenate(
            [lax.dot_general(qh_ref[pl.ds(pl.multiple_of(r * HG_BLK + c * CHUNK, CHUNK), CHUNK), :],
                             st_ref[r * cpb + c], nt, preferred_element_type=F32) for c in range(cpb)], axis=0)
        o = jnp.dot(a_ref[r % 2], v_ref[0, rows, :], preferred_element_type=F32) + inter
        ms = jnp.mean(o * o, axis=-1, keepdims=True)
        y = o * lax.rsqrt(ms + EPS) * nw
        z = z_ref[0, rows, :].astype(F32)
        o_ref[0, rows, :] = (y * (z * jax.nn.sigmoid(z))).astype(o_ref.dtype)

    scores(0)

    def pass_b(r, carry):
        outputs(r - 1)
        scores(r)
        return carry

    lax.fori_loop(1, nblk, pass_b, 0)
    outputs(nblk - 1)


def _hgrn(proj3, lb2, hnorm, tri_l, tri_u):
    b, l, _ = proj3.shape
    n = l // CHUNK
    cpb = HG_BLK // CHUNK
    c = lambda col: col // LANE
    seq = lambda col: pl.BlockSpec((1, l, LANE), lambda b_, h: (b_, 0, c(col) + h))
    blk = lambda: pl.BlockSpec((HG_BLK, HG_BLK), lambda b_, h: (0, 0))
    return pl.pallas_call(
        functools.partial(_hgrn_kernel, l=l),
        grid=(b, HB),
        in_specs=[seq(COL_BQ), seq(COL_BFF), seq(COL_BFB), seq(COL_BI), seq(COL_BZ),
                  pl.BlockSpec((2, LANE), lambda b_, h: (0, h)),
                  pl.BlockSpec((1, LANE), lambda b_, h: (0, 0)),
                  blk(), blk(), blk(), blk()],
        out_specs=pl.BlockSpec((1, l, LANE), lambda b_, h: (b_, 0, h)),
        out_shape=jax.ShapeDtypeStruct((b, l, HB * VB), BF16),
        scratch_shapes=[pltpu.VMEM((l, KB), BF16)] * 4
                       + [pltpu.VMEM((l, 2 * KB), BF16),
                          pltpu.VMEM((n, VB, 2 * KB), BF16),
                          pltpu.VMEM((VB, KB), F32), pltpu.VMEM((VB, KB), F32),
                          pltpu.VMEM((2, 2, HG_BLK, KB), BF16),
                          pltpu.VMEM((2, 2, cpb, 1, KB), F32),
                          pltpu.VMEM((2, HG_BLK, HG_BLK), BF16)],
        compiler_params=_cparams(2),
        name="hgrn2",
    )(proj3, proj3, proj3, proj3, proj3, lb2, hnorm, tri_l, tri_u,
      tri_l.astype(F32), tri_u.astype(F32))


def _mem_attn_kernel(q_ref, k_ref, v_ref, z_ref, o_ref):
    q = q_ref[0] * jnp.asarray(DC ** -0.5, BF16)
    s = lax.dot_general(q, k_ref[0], (((1,), (1,)), ((), ())), preferred_element_type=F32)
    m = jnp.max(s, axis=-1, keepdims=True)
    e = jnp.exp(s - m)
    p = e / jnp.sum(e, axis=-1, keepdims=True)
    o = jnp.dot(p.astype(BF16), v_ref[0], preferred_element_type=F32)
    z = z_ref[0].astype(F32)
    o_ref[0] = (o * (z * jax.nn.sigmoid(z))).astype(o_ref.dtype)


def _mem_attention(proj3, mkv3, tq):
    b, l, _ = proj3.shape
    cq, cz = COL_CQ // DC, COL_CZ // DC
    return pl.pallas_call(
        _mem_attn_kernel,
        grid=(b, HC, l // tq),
        in_specs=[pl.BlockSpec((1, tq, DC), lambda b_, h, i: (b_, i, cq + h)),
                  pl.BlockSpec((1, N_MEM, DC), lambda b_, h, i: (b_, 0, h)),
                  pl.BlockSpec((1, N_MEM, DC), lambda b_, h, i: (b_, 0, HC + h)),
                  pl.BlockSpec((1, tq, DC), lambda b_, h, i: (b_, i, cz + h))],
        out_specs=pl.BlockSpec((1, tq, DC), lambda b_, h, i: (b_, i, h)),
        out_shape=jax.ShapeDtypeStruct((b, l, HC * DC), BF16),
        compiler_params=_cparams(3),
        name="mem_attention",
    )(proj3, mkv3, mkv3, proj3)


def _merge_kernel(x_ref, ua_ref, ub_ref, uc_ref, ga_ref, gb_ref, gc_ref, wb_ref, wo_ref, pn_ref, o_ref):
    def branch(u_ref, g_ref, idx):
        y = jnp.dot(u_ref[...], wb_ref[idx], preferred_element_type=F32)
        return jax.nn.sigmoid(g_ref[...].astype(F32)) * y

    merged = branch(ua_ref, ga_ref, 0) + branch(ub_ref, gb_ref, 1) + branch(uc_ref, gc_ref, 2)
    y = jnp.dot(merged.astype(BF16), wo_ref[...], preferred_element_type=F32)
    ms = jnp.mean(y * y, axis=-1, keepdims=True)
    o_ref[...] = x_ref[...] + y * lax.rsqrt(ms + EPS) * pn_ref[...]


def _merge(x2d, ua, ub, uc, proj2, wb, wo, pn, tm):
    m, d = x2d.shape
    g0 = COL_G // d
    row = lambda: pl.BlockSpec((tm, d), lambda i: (i, 0))
    gate = lambda k: pl.BlockSpec((tm, d), lambda i: (i, g0 + k))
    return pl.pallas_call(
        _merge_kernel,
        grid=(m // tm,),
        in_specs=[row(), row(), row(), row(), gate(0), gate(1), gate(2),
                  pl.BlockSpec((3, d, d), lambda i: (0, 0, 0)),
                  pl.BlockSpec((d, d), lambda i: (0, 0)),
                  pl.BlockSpec((1, d), lambda i: (0, 0))],
        out_specs=row(),
        out_shape=jax.ShapeDtypeStruct((m, d), F32),
        compiler_params=_cparams(1),
        name="merge_out",
    )(x2d, ua, ub, uc, proj2, proj2, proj2, wb, wo, pn)


def _t5_bucket(rel):
    nb = NUM_BUCKETS // 2
    max_exact = nb // 2
    ret = jnp.where(rel > 0, nb, 0)
    n = jnp.abs(rel)
    nf = jnp.maximum(n, 1).astype(jnp.float32)
    large = max_exact + (jnp.log(nf / max_exact) / math.log(MAX_DISTANCE / max_exact)
                         * (nb - max_exact)).astype(jnp.int32)
    large = jnp.minimum(large, nb - 1)
    return ret + jnp.where(n < max_exact, n, large)


def _bias_blocks(rel_bias):
    n, w = BIAS_BLK, 2 * BIAS_BLK
    assert n >= MAX_DISTANCE
    k = jnp.arange(w, dtype=jnp.int32)
    k = jnp.where(k < n, k, k - w)
    o = jnp.arange(-(N_BIAS // 2), N_BIAS // 2 + 1, dtype=jnp.int32)
    u = rel_bias.astype(F32)[_t5_bucket(n * o[:, None] - k[None, :])] * LOG2E
    u = jnp.transpose(u, (2, 0, 1))
    flat = jnp.tile(u, (1, 1, n))[:, :, :n * (w - 1)]
    return flat.reshape(HA, N_BIAS, n, w - 1)[:, :, :, :n]


def _head_major(w):
    return w.reshape(D_MODEL, 2, HA, DA).transpose(0, 2, 1, 3).reshape(D_MODEL, 2 * HA * DA)


def kernel(x, mem, pre_norm, post_norm, w_in, lambda_q1, lambda_k1, lambda_q2, lambda_k2,
           diff_subln, rel_bias, lb_logits, hgrn_norm, mem_norm, w_mem_kv, w_branch, w_out):
    b, l, d = x.shape
    t = b * l
    layer = 0
    att_tq, att_tk = min(ATT_TQ, l), min(ATT_TK, l)

    lam_init = 0.8 - 0.6 * math.exp(-0.3 * layer)
    lam = (jnp.exp(jnp.sum(lambda_q1[layer].astype(F32) * lambda_k1[layer].astype(F32)))
           - jnp.exp(jnp.sum(lambda_q2[layer].astype(F32) * lambda_k2[layer].astype(F32)))
           + lam_init)
    bias_blk = _bias_blocks(rel_bias)
    scalars = lam.reshape(1).astype(F32)
    lb_all = jnp.cumsum(jax.nn.softmax(lb_logits.astype(F32), axis=1), axis=1)
    lb2 = lb_all[:, layer, :]
    w_l = w_in[layer]
    w_in_p = jnp.concatenate([_head_major(w_l[:, COL_AQ:COL_AK]).astype(BF16),
                              _head_major(w_l[:, COL_AK:COL_AV]).astype(BF16),
                              w_l[:, COL_AV:].astype(BF16)], axis=1)
    blk = np.arange(HG_BLK)
    same = (blk[:, None] // CHUNK) == (blk[None, :] // CHUNK)
    tri_l = jnp.asarray(same & (blk[None, :] <= blk[:, None]), BF16)
    tri_u = jnp.asarray(same & (blk[None, :] >= blk[:, None]), BF16)

    proj2 = _norm_matmul(x.reshape(t, d), pre_norm[layer].reshape(1, d), w_in_p, tm=min(1024, t), tn=2048)
    proj3 = proj2.reshape(b, l, IN_COLS)

    ua = _diff_attention(proj3, scalars, bias_blk, diff_subln[layer].reshape(1, 2 * DA).astype(F32),
                         att_tq, att_tk, 1.0 - lam_init)

    ub = _hgrn(proj3, lb2, hgrn_norm[layer].reshape(1, VB).astype(F32), tri_l, tri_u)

    mkv = _norm_matmul(mem.reshape(b * N_MEM, d), mem_norm[layer].reshape(1, d),
                       w_mem_kv[layer].astype(BF16), tm=min(1024, b * N_MEM), tn=2048)
    uc = _mem_attention(proj3, mkv.reshape(b, N_MEM, 2 * HC * DC), tq=min(1024, l))

    out = _merge(x.reshape(t, d), ua.reshape(t, d), ub.reshape(t, d), uc.reshape(t, d), proj2,
                 w_branch[layer].astype(BF16), w_out[layer].astype(BF16),
                 post_norm[layer].reshape(1, d).astype(F32), tm=min(512, t))
    return out.reshape(b, l, d)
```

```python
import functools
import math

import jax
import jax.numpy as jnp
import numpy as np
from jax import lax
from jax.experimental import pallas as pl
from jax.experimental.pallas import tpu as pltpu

F32 = jnp.float32
BF16 = jnp.bfloat16

D_MODEL = 1024
N_MEM = 256
HA, DA = 8, 64
HB, KB, VB = 8, 128, 128
HC, DC = 4, 256
NUM_BUCKETS, MAX_DISTANCE = 32, 128
CHUNK = 64
EPS = 1e-6
LANE = 128

COL_AQ, COL_AK, COL_AV, COL_AZ = 0, 1024, 2048, 3072
COL_BQ, COL_BFF, COL_BFB, COL_BI, COL_BZ = 4096, 5120, 6144, 7168, 8192
COL_CQ, COL_CZ, COL_G = 9216, 10240, 11264
IN_COLS = 14336

VMEM_LIMIT = 56 * 1024 * 1024


def _cparams(n_axes, flags=None):
    return pltpu.CompilerParams(dimension_semantics=("arbitrary",) * n_axes,
                                vmem_limit_bytes=VMEM_LIMIT, flags=flags)


def _norm_matmul_kernel(x_ref, nw_ref, w_ref, o_ref, h_ref):
    @pl.when(pl.program_id(1) == 0)
    def _():
        x = x_ref[...]
        ms = jnp.mean(x * x, axis=-1, keepdims=True)
        h_ref[...] = (x * lax.rsqrt(ms + EPS) * nw_ref[...]).astype(BF16)

    o_ref[...] = jnp.dot(h_ref[...], w_ref[...], preferred_element_type=F32).astype(o_ref.dtype)


def _norm_matmul(x2d, norm_w, w_bf16, tm, tn):
    m, d = x2d.shape
    n = w_bf16.shape[1]
    return pl.pallas_call(
        _norm_matmul_kernel,
        grid=(m // tm, n // tn),
        in_specs=[pl.BlockSpec((tm, d), lambda i, j: (i, 0)),
                  pl.BlockSpec((1, d), lambda i, j: (0, 0)),
                  pl.BlockSpec((d, tn), lambda i, j: (0, j))],
        out_specs=pl.BlockSpec((tm, tn), lambda i, j: (i, j)),
        out_shape=jax.ShapeDtypeStruct((m, n), BF16),
        scratch_shapes=[pltpu.VMEM((tm, d), BF16)],
        compiler_params=_cparams(2),
        name="norm_matmul",
    )(x2d, norm_w, w_bf16)


ATT_GROUP = 256
ATT_TQ, ATT_TK = 1024, 512
BIAS_BLK = 128
N_BIAS = 5
LOG2E = math.log2(math.e)
A_KEYS = 128
SUM_ROWS = 16


def _attn_kernel(sc_ref, q_ref, k_ref, v_ref, z_ref, bias_ref, sub_ref, o_ref,
                 vt_ref, qqt_ref, acc_ref, m_ref, s_ref, mx_ref, p_ref, a_ref,
                 *, tq, tk, nk, out_scale):
    i = pl.program_id(2)
    lam = sc_ref[0]
    ng = 2 * tq // ATT_GROUP
    gpm = tq // ATT_GROUP
    kblks, qblks = tk // BIAS_BLK, ATT_GROUP // BIAS_BLK

    @pl.when(i == 0)
    def _():
        for c in range(nk):
            vt_ref[c, 0:LANE] = v_ref[0, c * tk:(c + 1) * tk, :].astype(F32).T.astype(BF16)
            vt_ref[c, LANE:LANE + SUM_ROWS] = jnp.ones((SUM_ROWS, tk), BF16)

    qt = (q_ref[0].astype(F32) * (DA ** -0.5 * LOG2E)).T
    row = lax.broadcasted_iota(jnp.int32, (LANE, ATT_GROUP), 0)
    for g in range(ng):
        piece = qt[:, (g % gpm) * ATT_GROUP:(g % gpm + 1) * ATT_GROUP]
        keep = (row < DA) if g < gpm else (row >= DA)
        qqt_ref[g] = jnp.where(keep, piece, 0.0).astype(BF16)

    m_ref[...] = jnp.full(m_ref.shape, -jnp.inf, F32)
    acc_ref[...] = jnp.zeros(acc_ref.shape, F32)

    def bias_rows(j, g, kb):
        first = j * kblks - (i * (tq // BIAS_BLK) + (g % gpm) * qblks)
        return jnp.concatenate(
            [bias_ref[0, jnp.clip(first + kb - qb, -(N_BIAS // 2), N_BIAS // 2) + N_BIAS // 2]
             for qb in range(qblks)], axis=1)

    def stage_a(j, g):
        mx = None
        for kc in range(tk // A_KEYS):
            rows = pl.ds(pl.multiple_of(j * tk + kc * A_KEYS, A_KEYS), A_KEYS)
            kb, r0 = divmod(kc * A_KEYS, BIAS_BLK)
            s = (jnp.dot(k_ref[0, rows, :], qqt_ref[g], preferred_element_type=F32)
                 + bias_rows(j, g, kb)[r0:r0 + A_KEYS])
            s_ref[g % 2, kc * A_KEYS:(kc + 1) * A_KEYS] = s
            cur = jnp.max(s, axis=0, keepdims=True)
            mx = cur if mx is None else jnp.maximum(mx, cur)
        mx_ref[g % 2] = mx

    def stage_b(j, g):
        m_prev = m_ref[g]
        m_new = jnp.maximum(m_prev, mx_ref[g % 2])
        alpha = jnp.exp2(m_prev - m_new)
        p = jnp.exp2(s_ref[g % 2] - m_new)
        m_ref[g] = m_new
        a_ref[g % 2] = alpha
        p_ref[g % 2] = p.astype(BF16)

    def stage_c(j, g):
        acc_ref[g] = a_ref[g % 2] * acc_ref[g] + jnp.dot(vt_ref[j], p_ref[g % 2],
                                                         preferred_element_type=F32)

    def item_before(j, g, back):
        return (j, g - back) if g >= back else (j - 1, g - back + ng)

    def steps(j, first):
        for g in range(ng):
            if not (first and g < 2):
                stage_c(*item_before(j, g, 2))
            if not (first and g < 1):
                stage_b(*item_before(j, g, 1))
            stage_a(j, g)

    steps(0, True)

    def body(j, carry):
        steps(j, False)
        return carry

    lax.fori_loop(1, nk, body, 0)
    stage_c(nk - 1, ng - 2)
    stage_b(nk - 1, ng - 1)
    stage_c(nk - 1, ng - 1)

    def normalized(g):
        acc = acc_ref[g]
        return acc[0:LANE] / acc[LANE:LANE + 1]

    d = jnp.concatenate([normalized(g) - lam * normalized(g + gpm) for g in range(gpm)],
                        axis=1).T
    ms = jnp.mean(d * d, axis=-1, keepdims=True)
    y = d * lax.rsqrt(ms + EPS) * sub_ref[...] * out_scale
    z = z_ref[0].astype(F32)
    o_ref[0] = (y * (z * jax.nn.sigmoid(z))).astype(o_ref.dtype)


def _diff_attention(proj3, scalars, bias_blk, subln, tq, tk, out_scale):
    b, l, _ = proj3.shape
    nk = l // tk
    ng = 2 * tq // ATT_GROUP
    cq, ck, cv, cz = COL_AQ // LANE, COL_AK // LANE, COL_AV // LANE, COL_AZ // LANE
    kern = functools.partial(_attn_kernel, tq=tq, tk=tk, nk=nk, out_scale=out_scale)
    return pl.pallas_call(
        kern,
        grid=(b, HA, l // tq),
        in_specs=[pl.BlockSpec(memory_space=pltpu.SMEM),
                  pl.BlockSpec((1, tq, LANE), lambda b_, h, i: (b_, i, cq + h)),
                  pl.BlockSpec((1, l, LANE), lambda b_, h, i: (b_, 0, ck + h)),
                  pl.BlockSpec((1, l, LANE), lambda b_, h, i: (b_, 0, cv + h)),
                  pl.BlockSpec((1, tq, LANE), lambda b_, h, i: (b_, i, cz + h)),
                  pl.BlockSpec((1, N_BIAS, BIAS_BLK, BIAS_BLK), lambda b_, h, i: (h, 0, 0, 0)),
                  pl.BlockSpec((1, LANE), lambda b_, h, i: (0, 0))],
        out_specs=pl.BlockSpec((1, tq, LANE), lambda b_, h, i: (b_, i, h)),
        out_shape=jax.ShapeDtypeStruct((b, l, HA * 2 * DA), BF16),
        scratch_shapes=[pltpu.VMEM((nk, LANE + SUM_ROWS, tk), BF16),
                        pltpu.VMEM((ng, LANE, ATT_GROUP), BF16),
                        pltpu.VMEM((ng, LANE + SUM_ROWS, ATT_GROUP), F32),
                        pltpu.VMEM((ng, 1, ATT_GROUP), F32),
                        pltpu.VMEM((2, tk, ATT_GROUP), F32),
                        pltpu.VMEM((2, 1, ATT_GROUP), F32),
                        pltpu.VMEM((2, tk, ATT_GROUP), BF16),
                        pltpu.VMEM((2, 1, ATT_GROUP), F32)],
        compiler_params=_cparams(3),
        name="diff_attention",
    )(scalars, proj3, proj3, proj3, proj3, bias_blk, subln)


HG_BLK = 256


def _hgrn_kernel(q_ref, ff_ref, fb_ref, v_ref, z_ref, lb_ref, nw_ref, tl_ref, tu_ref, ml_ref, mu_ref, o_ref,
                 qtf_ref, ktf_ref, qtb_ref, ktb_ref, qh_ref, st_ref, sf_ref, sb_ref,
                 kh_ref, dec_ref, a_ref, *, l):
    nblk = l // HG_BLK
    cpb = HG_BLK // CHUNK
    tn = (((0,), (0,)), ((), ()))
    nt = (((1,), (1,)), ((), ()))
    dirs = ((ff_ref, 0, tl_ref, CHUNK // 2 - 1, CHUNK - 1, qtf_ref, ktf_ref, sf_ref),
            (fb_ref, 1, tu_ref, CHUNK // 2, 0, qtb_ref, ktb_ref, sb_ref))

    def block_of(r, di):
        return r if di == 0 else nblk - 1 - r

    def cumsum(tri, g):
        g_hi = g.astype(BF16)
        g_lo = (g - g_hi.astype(F32)).astype(BF16)
        return (jnp.dot(tri, g_hi, preferred_element_type=F32)
                + jnp.dot(tri, g_lo, preferred_element_type=F32))

    def prep(r):
        for f_ref, di, tri_ref, mid_row, last_row, qt_ref, kt_ref, _ in dirs:
            rows = pl.ds(pl.multiple_of(block_of(r, di) * HG_BLK, HG_BLK), HG_BLK)
            lb = lb_ref[di:di + 1, :]
            x = f_ref[0, rows, :].astype(F32)
            q3 = q_ref[0, rows, :].astype(F32).reshape(cpb, CHUNK, LANE)
            f = lb + (1.0 - lb) * jax.nn.sigmoid(x)
            g = jnp.log(f)
            k3 = (1.0 - f).reshape(cpb, CHUNK, LANE)
            bcs = cumsum(tri_ref[...], g).reshape(cpb, CHUNK, LANE)
            b_mid = bcs[:, mid_row:mid_row + 1, :]
            b_last = bcs[:, last_row:last_row + 1, :]
            qt = q3 * jnp.exp(bcs - b_mid)
            kt = k3 * jnp.exp(b_mid - bcs)
            qh = qt * jnp.exp(b_mid)
            kh = kt * jnp.exp(b_last - b_mid)
            qt_ref[rows, :] = qt.reshape(HG_BLK, LANE).astype(BF16)
            kt_ref[rows, :] = kt.reshape(HG_BLK, LANE).astype(BF16)
            qh_ref[rows, di * KB:(di + 1) * KB] = qh.reshape(HG_BLK, LANE).astype(BF16)
            kh_ref[r % 2, di] = kh.reshape(HG_BLK, LANE).astype(BF16)
            dec_ref[r % 2, di] = jnp.exp(b_last)

    def scan(r):
        for _, di, _, _, _, _, _, s_ref in dirs:
            blk = block_of(r, di)
            s = s_ref[...]
            for c in (range(cpb) if di == 0 else reversed(range(cpb))):
                st_ref[blk * cpb + c, :, di * KB:(di + 1) * KB] = s.astype(BF16)
                vrows = pl.ds(pl.multiple_of(blk * HG_BLK + c * CHUNK, CHUNK), CHUNK)
                u = lax.dot_general(v_ref[0, vrows, :], kh_ref[r % 2, di, c * CHUNK:(c + 1) * CHUNK, :], tn,
                                    preferred_element_type=F32)
                s = s * dec_ref[r % 2, di, c] + u
            s_ref[...] = s

    sf_ref[...] = jnp.zeros(sf_ref.shape, F32)
    sb_ref[...] = jnp.zeros(sb_ref.shape, F32)
    prep(0)

    def pass_a(r, carry):
        scan(r - 1)
        prep(r)
        return carry

    lax.fori_loop(1, nblk, pass_a, 0)
    scan(nblk - 1)

    nw = nw_ref[...]

    def scores(r):
        rows = pl.ds(pl.multiple_of(r * HG_BLK, HG_BLK), HG_BLK)
        a_f = lax.dot_general(qtf_ref[rows, :], ktf_ref[rows, :], nt, preferred_element_type=F32)
        a_b = lax.dot_general(qtb_ref[rows, :], ktb_ref[rows, :], nt, preferred_element_type=F32)
        a = jnp.where(ml_ref[...] != 0.0, a_f, 0.0) + jnp.where(mu_ref[...] != 0.0, a_b, 0.0)
        a_ref[r % 2] = a.astype(BF16)

    def outputs(r):
        rows = pl.ds(pl.multiple_of(r * HG_BLK, HG_BLK), HG_BLK)
        inter = jnp.concatenate(
            [lax.dot_general(qh_ref[pl.ds(pl.multiple_of(r * HG_BLK + c * CHUNK, CHUNK), CHUNK), :],
                             st_ref[r * cpb + c], nt, preferred_element_type=F32) for c in range(cpb)], axis=0)
        o = jnp.dot(a_ref[r % 2], v_ref[0, rows, :], preferred_element_type=F32) + inter
        ms = jnp.mean(o * o, axis=-1, keepdims=True)
        y = o * lax.rsqrt(ms + EPS) * nw
        z = z_ref[0, rows, :].astype(F32)
        o_ref[0, rows, :] = (y * (z * jax.nn.sigmoid(z))).astype(o_ref.dtype)

    scores(0)

    def pass_b(r, carry):
        outputs(r - 1)
        scores(r)
        return carry

    lax.fori_loop(1, nblk, pass_b, 0)
    outputs(nblk - 1)


def _hgrn(proj3, lb2, hnorm, tri_l, tri_u):
    b, l, _ = proj3.shape
    n = l // CHUNK
    cpb = HG_BLK // CHUNK
    c = lambda col: col // LANE
    seq = lambda col: pl.BlockSpec((1, l, LANE), lambda b_, h: (b_, 0, c(col) + h))
    blk = lambda: pl.BlockSpec((HG_BLK, HG_BLK), lambda b_, h: (0, 0))
    return pl.pallas_call(
        functools.partial(_hgrn_kernel, l=l),
        grid=(b, HB),
        in_specs=[seq(COL_BQ), seq(COL_BFF), seq(COL_BFB), seq(COL_BI), seq(COL_BZ),
                  pl.BlockSpec((2, LANE), lambda b_, h: (0, h)),
                  pl.BlockSpec((1, LANE), lambda b_, h: (0, 0)),
                  blk(), blk(), blk(), blk()],
        out_specs=pl.BlockSpec((1, l, LANE), lambda b_, h: (b_, 0, h)),
        out_shape=jax.ShapeDtypeStruct((b, l, HB * VB), BF16),
        scratch_shapes=[pltpu.VMEM((l, KB), BF16)] * 4
                       + [pltpu.VMEM((l, 2 * KB), BF16),
                          pltpu.VMEM((n, VB, 2 * KB), BF16),
                          pltpu.VMEM((VB, KB), F32), pltpu.VMEM((VB, KB), F32),
                          pltpu.VMEM((2, 2, HG_BLK, KB), BF16),
                          pltpu.VMEM((2, 2, cpb, 1, KB), F32),
                          pltpu.VMEM((2, HG_BLK, HG_BLK), BF16)],
        compiler_params=_cparams(2),
        name="hgrn2",
    )(proj3, proj3, proj3, proj3, proj3, lb2, hnorm, tri_l, tri_u,
      tri_l.astype(F32), tri_u.astype(F32))


def _mem_attn_kernel(q_ref, k_ref, v_ref, z_ref, o_ref):
    q = q_ref[0] * jnp.asarray(DC ** -0.5, BF16)
    s = lax.dot_general(q, k_ref[0], (((1,), (1,)), ((), ())), preferred_element_type=F32)
    m = jnp.max(s, axis=-1, keepdims=True)
    e = jnp.exp(s - m)
    p = e / jnp.sum(e, axis=-1, keepdims=True)
    o = jnp.dot(p.astype(BF16), v_ref[0], preferred_element_type=F32)
    z = z_ref[0].astype(F32)
    o_ref[0] = (o * (z * jax.nn.sigmoid(z))).astype(o_ref.dtype)


def _mem_attention(proj3, mkv3, tq):
    b, l, _ = proj3.shape
    cq, cz = COL_CQ // DC, COL_CZ // DC
    return pl.pallas_call(
        _mem_attn_kernel,
        grid=(b, HC, l // tq),
        in_specs=[pl.BlockSpec((1, tq, DC), lambda b_, h, i: (b_, i, cq + h)),
                  pl.BlockSpec((1, N_MEM, DC), lambda b_, h, i: (b_, 0, h)),
                  pl.BlockSpec((1, N_MEM, DC), lambda b_, h, i: (b_, 0, HC + h)),
                  pl.BlockSpec((1, tq, DC), lambda b_, h, i: (b_, i, cz + h))],
        out_specs=pl.BlockSpec((1, tq, DC), lambda b_, h, i: (b_, i, h)),
        out_shape=jax.ShapeDtypeStruct((b, l, HC * DC), BF16),
        compiler_params=_cparams(3),
        name="mem_attention",
    )(proj3, mkv3, mkv3, proj3)


def _merge_kernel(x_ref, ua_ref, ub_ref, uc_ref, ga_ref, gb_ref, gc_ref, wb_ref, wo_ref, pn_ref, o_ref):
    def branch(u_ref, g_ref, idx):
        y = jnp.dot(u_ref[...], wb_ref[idx], preferred_element_type=F32)
        return jax.nn.sigmoid(g_ref[...].astype(F32)) * y

    merged = branch(ua_ref, ga_ref, 0) + branch(ub_ref, gb_ref, 1) + branch(uc_ref, gc_ref, 2)
    y = jnp.dot(merged.astype(BF16), wo_ref[...], preferred_element_type=F32)
    ms = jnp.mean(y * y, axis=-1, keepdims=True)
    o_ref[...] = x_ref[...] + y * lax.rsqrt(ms + EPS) * pn_ref[...]


def _merge(x2d, ua, ub, uc, proj2, wb, wo, pn, tm):
    m, d = x2d.shape
    g0 = COL_G // d
    row = lambda: pl.BlockSpec((tm, d), lambda i: (i, 0))
    gate = lambda k: pl.BlockSpec((tm, d), lambda i: (i, g0 + k))
    return pl.pallas_call(
        _merge_kernel,
        grid=(m // tm,),
        in_specs=[row(), row(), row(), row(), gate(0), gate(1), gate(2),
                  pl.BlockSpec((3, d, d), lambda i: (0, 0, 0)),
                  pl.BlockSpec((d, d), lambda i: (0, 0)),
                  pl.BlockSpec((1, d), lambda i: (0, 0))],
        out_specs=row(),
        out_shape=jax.ShapeDtypeStruct((m, d), F32),
        compiler_params=_cparams(1),
        name="merge_out",
    )(x2d, ua, ub, uc, proj2, proj2, proj2, wb, wo, pn)


def _t5_bucket(rel):
    nb = NUM_BUCKETS // 2
    max_exact = nb // 2
    ret = jnp.where(rel > 0, nb, 0)
    n = jnp.abs(rel)
    nf = jnp.maximum(n, 1).astype(jnp.float32)
    large = max_exact + (jnp.log(nf / max_exact) / math.log(MAX_DISTANCE / max_exact)
                         * (nb - max_exact)).astype(jnp.int32)
    large = jnp.minimum(large, nb - 1)
    return ret + jnp.where(n < max_exact, n, large)


def _bias_blocks(rel_bias):
    n, w = BIAS_BLK, 2 * BIAS_BLK
    assert n >= MAX_DISTANCE
    k = jnp.arange(w, dtype=jnp.int32)
    k = jnp.where(k < n, k, k - w)
    o = jnp.arange(-(N_BIAS // 2), N_BIAS // 2 + 1, dtype=jnp.int32)
    u = rel_bias.astype(F32)[_t5_bucket(n * o[:, None] - k[None, :])] * LOG2E
    u = jnp.transpose(u, (2, 0, 1))
    flat = jnp.tile(u, (1, 1, n))[:, :, :n * (w - 1)]
    return flat.reshape(HA, N_BIAS, n, w - 1)[:, :, :, :n]


def _head_major(w):
    return w.reshape(D_MODEL, 2, HA, DA).transpose(0, 2, 1, 3).reshape(D_MODEL, 2 * HA * DA)


def kernel(x, mem, pre_norm, post_norm, w_in, lambda_q1, lambda_k1, lambda_q2, lambda_k2,
           diff_subln, rel_bias, lb_logits, hgrn_norm, mem_norm, w_mem_kv, w_branch, w_out):
    b, l, d = x.shape
    t = b * l
    layer = 0
    att_tq, att_tk = min(ATT_TQ, l), min(ATT_TK, l)

    lam_init = 0.8 - 0.6 * math.exp(-0.3 * layer)
    lam = (jnp.exp(jnp.sum(lambda_q1[layer].astype(F32) * lambda_k1[layer].astype(F32)))
           - jnp.exp(jnp.sum(lambda_q2[layer].astype(F32) * lambda_k2[layer].astype(F32)))
           + lam_init)
    bias_blk = _bias_blocks(rel_bias)
    scalars = lam.reshape(1).astype(F32)
    lb_all = jnp.cumsum(jax.nn.softmax(lb_logits.astype(F32), axis=1), axis=1)
    lb2 = lb_all[:, layer, :]
    w_l = w_in[layer]
    w_in_p = jnp.concatenate([_head_major(w_l[:, COL_AQ:COL_AK]).astype(BF16),
                              _head_major(w_l[:, COL_AK:COL_AV]).astype(BF16),
                              w_l[:, COL_AV:].astype(BF16)], axis=1)
    blk = np.arange(HG_BLK)
    same = (blk[:, None] // CHUNK) == (blk[None, :] // CHUNK)
    tri_l = jnp.asarray(same & (blk[None, :] <= blk[:, None]), BF16)
    tri_u = jnp.asarray(same & (blk[None, :] >= blk[:, None]), BF16)

    proj2 = _norm_matmul(x.reshape(t, d), pre_norm[layer].reshape(1, d), w_in_p, tm=min(1024, t), tn=2048)
    proj3 = proj2.reshape(b, l, IN_COLS)

    ua = _diff_attention(proj3, scalars, bias_blk, diff_subln[layer].reshape(1, 2 * DA).astype(F32),
                         att_tq, att_tk, 1.0 - lam_init)

    ub = _hgrn(proj3, lb2, hgrn_norm[layer].reshape(1, VB).astype(F32), tri_l, tri_u)

    mkv = _norm_matmul(mem.reshape(b * N_MEM, d), mem_norm[layer].reshape(1, d),
                       w_mem_kv[layer].astype(BF16), tm=min(1024, b * N_MEM), tn=2048)
    uc = _mem_attention(proj3, mkv.reshape(b, N_MEM, 2 * HC * DC), tq=min(1024, l))

    out = _merge(x.reshape(t, d), ua.reshape(t, d), ub.reshape(t, d), uc.reshape(t, d), proj2,
                 w_branch[layer].astype(BF16), w_out[layer].astype(BF16),
                 post_norm[layer].reshape(1, d).astype(F32), tm=min(512, t))
    return out.reshape(b, l, d)
```

```python
import functools
import math

import jax
import jax.numpy as jnp
import numpy as np
from jax import lax
from jax.experimental import pallas as pl
from jax.experimental.pallas import tpu as pltpu

F32 = jnp.float32
BF16 = jnp.bfloat16

D_MODEL = 1024
N_MEM = 256
HA, DA = 8, 64
HB, KB, VB = 8, 128, 128
HC, DC = 4, 256
NUM_BUCKETS, MAX_DISTANCE = 32, 128
CHUNK = 64
EPS = 1e-6
LANE = 128

COL_AQ, COL_AK, COL_AV, COL_AZ = 0, 1024, 2048, 3072
COL_BQ, COL_BFF, COL_BFB, COL_BI, COL_BZ = 4096, 5120, 6144, 7168, 8192
COL_CQ, COL_CZ, COL_G = 9216, 10240, 11264
IN_COLS = 14336

VMEM_LIMIT = 56 * 1024 * 1024


def _cparams(n_axes, flags=None):
    return pltpu.CompilerParams(dimension_semantics=("arbitrary",) * n_axes,
                                vmem_limit_bytes=VMEM_LIMIT, flags=flags)


def _norm_matmul_kernel(x_ref, nw_ref, w_ref, o_ref, h_ref):
    @pl.when(pl.program_id(1) == 0)
    def _():
        x = x_ref[...]
        ms = jnp.mean(x * x, axis=-1, keepdims=True)
        h_ref[...] = (x * lax.rsqrt(ms + EPS) * nw_ref[...]).astype(BF16)

    o_ref[...] = jnp.dot(h_ref[...], w_ref[...], preferred_element_type=F32).astype(o_ref.dtype)


def _norm_matmul(x2d, norm_w, w_bf16, tm, tn):
    m, d = x2d.shape
    n = w_bf16.shape[1]
    return pl.pallas_call(
        _norm_matmul_kernel,
        grid=(m // tm, n // tn),
        in_specs=[pl.BlockSpec((tm, d), lambda i, j: (i, 0)),
                  pl.BlockSpec((1, d), lambda i, j: (0, 0)),
                  pl.BlockSpec((d, tn), lambda i, j: (0, j))],
        out_specs=pl.BlockSpec((tm, tn), lambda i, j: (i, j)),
        out_shape=jax.ShapeDtypeStruct((m, n), BF16),
        scratch_shapes=[pltpu.VMEM((tm, d), BF16)],
        compiler_params=_cparams(2),
        name="norm_matmul",
    )(x2d, norm_w, w_bf16)


ATT_GROUP = 256
ATT_TQ, ATT_TK = 1024, 512
BIAS_BLK = 128
N_BIAS = 5
LOG2E = math.log2(math.e)
SUM_ROWS = 16


def _attn_kernel(sc_ref, q_ref, k_ref, v_ref, z_ref, bias_ref, sub_ref, o_ref,
                 vt_ref, qqt_ref, acc_ref, m_ref, s_ref, mx_ref, c_ref, p_ref, a_ref,
                 *, tq, tk, nk, out_scale):
    h = pl.program_id(1)
    i = pl.program_id(2)
    lam = sc_ref[0]
    ng = 2 * tq // ATT_GROUP
    gpm = tq // ATT_GROUP
    kblks, qblks = tk // BIAS_BLK, ATT_GROUP // BIAS_BLK

    @pl.when(i == 0)
    def _():
        for c in range(nk):
            vt_ref[c, 0:LANE] = v_ref[0, c * tk:(c + 1) * tk, :].astype(F32).T.astype(BF16)
            vt_ref[c, LANE:LANE + SUM_ROWS] = jnp.ones((SUM_ROWS, tk), BF16)

    qt = (q_ref[0].astype(F32) * (DA ** -0.5 * LOG2E)).T
    row = lax.broadcasted_iota(jnp.int32, (LANE, ATT_GROUP), 0)
    for g in range(ng):
        piece = qt[:, (g % gpm) * ATT_GROUP:(g % gpm + 1) * ATT_GROUP]
        keep = (row < DA) if g < gpm else (row >= DA)
        qqt_ref[g] = jnp.where(keep, piece, 0.0).astype(BF16)

    m_ref[...] = jnp.full(m_ref.shape, -jnp.inf, F32)
    acc_ref[...] = jnp.zeros(acc_ref.shape, F32)

    half = N_BIAS // 2
    c_left, c_right = sc_ref[1 + 2 * h], sc_ref[2 + 2 * h]
    ratio = (tq // BIAS_BLK) // kblks

    def stage_a(j, g, mode):
        qb0 = (g % gpm) * qblks
        mx = None
        for kb in range(kblks):
            rows = pl.ds(pl.multiple_of(j * tk + kb * BIAS_BLK, BIAS_BLK), BIAS_BLK)
            s = jnp.dot(k_ref[0, rows, :], qqt_ref[g], preferred_element_type=F32)
            cols, consts = [], []
            for qb in range(qblks):
                blk = s[:, qb * BIAS_BLK:(qb + 1) * BIAS_BLK]
                if mode == "table":
                    o = j * kblks + kb - (i * (tq // BIAS_BLK) + qb0 + qb)
                    blk, c = blk + bias_ref[0, jnp.clip(o, -half, half) + half], 0.0
                elif mode == "left":
                    c = c_left
                elif mode == "right":
                    c = c_right
                else:
                    o = mode * kblks + kb - (qb0 + qb)
                    if abs(o) < half:
                        blk, c = blk + bias_ref[0, o + half], 0.0
                    else:
                        c = c_left if o < 0 else c_right
                cols.append(blk)
                consts.append(jnp.full((1, BIAS_BLK), c, F32))
            s = jnp.concatenate(cols, axis=1)
            cvec = jnp.concatenate(consts, axis=1)
            s_ref[g % 2, kb * BIAS_BLK:(kb + 1) * BIAS_BLK] = s
            c_ref[g % 2, kb] = cvec
            cur = jnp.max(s, axis=0, keepdims=True) + cvec
            mx = cur if mx is None else jnp.maximum(mx, cur)
        mx_ref[g % 2] = mx

    def stage_b(j, g):
        m_prev = m_ref[g]
        m_new = jnp.maximum(m_prev, mx_ref[g % 2])
        m_ref[g] = m_new
        a_ref[g % 2] = jnp.exp2(m_prev - m_new)
        for kb in range(kblks):
            rows = slice(kb * BIAS_BLK, (kb + 1) * BIAS_BLK)
            p_ref[g % 2, rows] = jnp.exp2(s_ref[g % 2, rows] - (m_new - c_ref[g % 2, kb])).astype(BF16)

    def stage_c(j, g):
        acc_ref[g] = a_ref[g % 2] * acc_ref[g] + jnp.dot(vt_ref[j], p_ref[g % 2],
                                                         preferred_element_type=F32)

    def item_before(j, g, back):
        return (j, g - back) if g >= back else (j - 1, g - back + ng)

    def steps(j, mode, first=False):
        for g in range(ng):
            if not (first and g < 2):
                stage_c(*item_before(j, g, 2))
            if not (first and g < 1):
                stage_b(*item_before(j, g, 1))
            stage_a(j, g, mode)

    def tiles(lo, hi, mode):
        def body(j, carry):
            steps(j, mode)
            return carry
        lax.fori_loop(lo, hi, body, 0)

    steps(0, "table", first=True)
    tiles(1, jnp.maximum(ratio * i - 1, 1), "left")
    for w in range(-1, ratio + 1):
        j_w = ratio * i + w

        @pl.when((j_w >= 1) & (j_w < nk))
        def _():
            steps(j_w, w)
    tiles(jnp.maximum(ratio * i + ratio + 1, 1), nk, "right")
    stage_c(nk - 1, ng - 2)
    stage_b(nk - 1, ng - 1)
    stage_c(nk - 1, ng - 1)

    def normalized(g):
        acc = acc_ref[g]
        return acc[0:LANE] / acc[LANE:LANE + 1]

    d = jnp.concatenate([normalized(g) - lam * normalized(g + gpm) for g in range(gpm)],
                        axis=1).T
    ms = jnp.mean(d * d, axis=-1, keepdims=True)
    y = d * lax.rsqrt(ms + EPS) * sub_ref[...] * out_scale
    z = z_ref[0].astype(F32)
    o_ref[0] = (y * (z * jax.nn.sigmoid(z))).astype(o_ref.dtype)


def _diff_attention(proj3, scalars, bias_blk, subln, tq, tk, out_scale):
    b, l, _ = proj3.shape
    nk = l // tk
    ng = 2 * tq // ATT_GROUP
    cq, ck, cv, cz = COL_AQ // LANE, COL_AK // LANE, COL_AV // LANE, COL_AZ // LANE
    kern = functools.partial(_attn_kernel, tq=tq, tk=tk, nk=nk, out_scale=out_scale)
    return pl.pallas_call(
        kern,
        grid=(b, HA, l // tq),
        in_specs=[pl.BlockSpec(memory_space=pltpu.SMEM),
                  pl.BlockSpec((1, tq, LANE), lambda b_, h, i: (b_, i, cq + h)),
                  pl.BlockSpec((1, l, LANE), lambda b_, h, i: (b_, 0, ck + h)),
                  pl.BlockSpec((1, l, LANE), lambda b_, h, i: (b_, 0, cv + h)),
                  pl.BlockSpec((1, tq, LANE), lambda b_, h, i: (b_, i, cz + h)),
                  pl.BlockSpec((1, N_BIAS, BIAS_BLK, BIAS_BLK), lambda b_, h, i: (h, 0, 0, 0)),
                  pl.BlockSpec((1, LANE), lambda b_, h, i: (0, 0))],
        out_specs=pl.BlockSpec((1, tq, LANE), lambda b_, h, i: (b_, i, h)),
        out_shape=jax.ShapeDtypeStruct((b, l, HA * 2 * DA), BF16),
        scratch_shapes=[pltpu.VMEM((nk, LANE + SUM_ROWS, tk), BF16),
                        pltpu.VMEM((ng, LANE, ATT_GROUP), BF16),
                        pltpu.VMEM((ng, LANE + SUM_ROWS, ATT_GROUP), F32),
                        pltpu.VMEM((ng, 1, ATT_GROUP), F32),
                        pltpu.VMEM((2, tk, ATT_GROUP), F32),
                        pltpu.VMEM((2, 1, ATT_GROUP), F32),
                        pltpu.VMEM((2, tk // BIAS_BLK, 1, ATT_GROUP), F32),
                        pltpu.VMEM((2, tk, ATT_GROUP), BF16),
                        pltpu.VMEM((2, 1, ATT_GROUP), F32)],
        compiler_params=_cparams(3),
        name="diff_attention",
    )(scalars, proj3, proj3, proj3, proj3, bias_blk, subln)


HG_BLK = 256


def _hgrn_kernel(q_ref, ff_ref, fb_ref, v_ref, z_ref, lb_ref, nw_ref, tl_ref, tu_ref, ml_ref, mu_ref, o_ref,
                 qtf_ref, ktf_ref, qtb_ref, ktb_ref, qh_ref, st_ref, sf_ref, sb_ref,
                 kh_ref, dec_ref, a_ref, *, l):
    nblk = l // HG_BLK
    cpb = HG_BLK // CHUNK
    tn = (((0,), (0,)), ((), ()))
    nt = (((1,), (1,)), ((), ()))
    dirs = ((ff_ref, 0, tl_ref, CHUNK // 2 - 1, CHUNK - 1, qtf_ref, ktf_ref, sf_ref),
            (fb_ref, 1, tu_ref, CHUNK // 2, 0, qtb_ref, ktb_ref, sb_ref))

    def block_of(r, di):
        return r if di == 0 else nblk - 1 - r

    def cumsum(tri, g):
        g_hi = g.astype(BF16)
        g_lo = (g - g_hi.astype(F32)).astype(BF16)
        return (jnp.dot(tri, g_hi, preferred_element_type=F32)
                + jnp.dot(tri, g_lo, preferred_element_type=F32))

    def prep(r):
        for f_ref, di, tri_ref, mid_row, last_row, qt_ref, kt_ref, _ in dirs:
            rows = pl.ds(pl.multiple_of(block_of(r, di) * HG_BLK, HG_BLK), HG_BLK)
            lb = lb_ref[di:di + 1, :]
            x = f_ref[0, rows, :].astype(F32)
            q3 = q_ref[0, rows, :].astype(F32).reshape(cpb, CHUNK, LANE)
            f = lb + (1.0 - lb) * jax.nn.sigmoid(x)
            g = jnp.log(f)
            k3 = (1.0 - f).reshape(cpb, CHUNK, LANE)
            bcs = cumsum(tri_ref[...], g).reshape(cpb, CHUNK, LANE)
            b_mid = bcs[:, mid_row:mid_row + 1, :]
            b_last = bcs[:, last_row:last_row + 1, :]
            qt = q3 * jnp.exp(bcs - b_mid)
            kt = k3 * jnp.exp(b_mid - bcs)
            qh = qt * jnp.exp(b_mid)
            kh = kt * jnp.exp(b_last - b_mid)
            qt_ref[rows, :] = qt.reshape(HG_BLK, LANE).astype(BF16)
            kt_ref[rows, :] = kt.reshape(HG_BLK, LANE).astype(BF16)
            qh_ref[rows, di * KB:(di + 1) * KB] = qh.reshape(HG_BLK, LANE).astype(BF16)
            kh_ref[r % 2, di] = kh.reshape(HG_BLK, LANE).astype(BF16)
            dec_ref[r % 2, di] = jnp.exp(b_last)

    def scan(r):
        for _, di, _, _, _, _, _, s_ref in dirs:
            blk = block_of(r, di)
            s = s_ref[...]
            for c in (range(cpb) if di == 0 else reversed(range(cpb))):
                st_ref[blk * cpb + c, :, di * KB:(di + 1) * KB] = s.astype(BF16)
                vrows = pl.ds(pl.multiple_of(blk * HG_BLK + c * CHUNK, CHUNK), CHUNK)
                u = lax.dot_general(v_ref[0, vrows, :], kh_ref[r % 2, di, c * CHUNK:(c + 1) * CHUNK, :], tn,
                                    preferred_element_type=F32)
                s = s * dec_ref[r % 2, di, c] + u
            s_ref[...] = s

    sf_ref[...] = jnp.zeros(sf_ref.shape, F32)
    sb_ref[...] = jnp.zeros(sb_ref.shape, F32)
    prep(0)

    def pass_a(r, carry):
        scan(r - 1)
        prep(r)
        return carry

    lax.fori_loop(1, nblk, pass_a, 0)
    scan(nblk - 1)

    nw = nw_ref[...]

    def scores(r):
        rows = pl.ds(pl.multiple_of(r * HG_BLK, HG_BLK), HG_BLK)
        a_f = lax.dot_general(qtf_ref[rows, :], ktf_ref[rows, :], nt, preferred_element_type=F32)
        a_b = lax.dot_general(qtb_ref[rows, :], ktb_ref[rows, :], nt, preferred_element_type=F32)
        a = jnp.where(ml_ref[...] != 0.0, a_f, 0.0) + jnp.where(mu_ref[...] != 0.0, a_b, 0.0)
        a_ref[r % 2] = a.astype(BF16)

    def outputs(r):
        rows = pl.ds(pl.multiple_of(r * HG_BLK, HG_BLK), HG_BLK)
        inter = jnp.concatenate(
            [lax.dot_general(qh_ref[pl.ds(pl.multiple_of(r * HG_BLK + c * CHUNK, CHUNK), CHUNK), :],
                             st_ref[r * cpb + c], nt, preferred_element_type=F32) for c in range(cpb)], axis=0)
        o = jnp.dot(a_ref[r % 2], v_ref[0, rows, :], preferred_element_type=F32) + inter
        ms = jnp.mean(o * o, axis=-1, keepdims=True)
        y = o * lax.rsqrt(ms + EPS) * nw
        z = z_ref[0, rows, :].astype(F32)
        o_ref[0, rows, :] = (y * (z * jax.nn.sigmoid(z))).astype(o_ref.dtype)

    scores(0)

    def pass_b(r, carry):
        outputs(r - 1)
        scores(r)
        return carry

    lax.fori_loop(1, nblk, pass_b, 0)
    outputs(nblk - 1)


def _hgrn(proj3, lb2, hnorm, tri_l, tri_u):
    b, l, _ = proj3.shape
    n = l // CHUNK
    cpb = HG_BLK // CHUNK
    c = lambda col: col // LANE
    seq = lambda col: pl.BlockSpec((1, l, LANE), lambda b_, h: (b_, 0, c(col) + h))
    blk = lambda: pl.BlockSpec((HG_BLK, HG_BLK), lambda b_, h: (0, 0))
    return pl.pallas_call(
        functools.partial(_hgrn_kernel, l=l),
        grid=(b, HB),
        in_specs=[seq(COL_BQ), seq(COL_BFF), seq(COL_BFB), seq(COL_BI), seq(COL_BZ),
                  pl.BlockSpec((2, LANE), lambda b_, h: (0, h)),
                  pl.BlockSpec((1, LANE), lambda b_, h: (0, 0)),
                  blk(), blk(), blk(), blk()],
        out_specs=pl.BlockSpec((1, l, LANE), lambda b_, h: (b_, 0, h)),
        out_shape=jax.ShapeDtypeStruct((b, l, HB * VB), BF16),
        scratch_shapes=[pltpu.VMEM((l, KB), BF16)] * 4
                       + [pltpu.VMEM((l, 2 * KB), BF16),
                          pltpu.VMEM((n, VB, 2 * KB), BF16),
                          pltpu.VMEM((VB, KB), F32), pltpu.VMEM((VB, KB), F32),
                          pltpu.VMEM((2, 2, HG_BLK, KB), BF16),
                          pltpu.VMEM((2, 2, cpb, 1, KB), F32),
                          pltpu.VMEM((2, HG_BLK, HG_BLK), BF16)],
        compiler_params=_cparams(2),
        name="hgrn2",
    )(proj3, proj3, proj3, proj3, proj3, lb2, hnorm, tri_l, tri_u,
      tri_l.astype(F32), tri_u.astype(F32))


def _mem_attn_kernel(q_ref, k_ref, v_ref, z_ref, o_ref):
    q = q_ref[0] * jnp.asarray(DC ** -0.5, BF16)
    s = lax.dot_general(q, k_ref[0], (((1,), (1,)), ((), ())), preferred_element_type=F32)
    m = jnp.max(s, axis=-1, keepdims=True)
    e = jnp.exp(s - m)
    p = e / jnp.sum(e, axis=-1, keepdims=True)
    o = jnp.dot(p.astype(BF16), v_ref[0], preferred_element_type=F32)
    z = z_ref[0].astype(F32)
    o_ref[0] = (o * (z * jax.nn.sigmoid(z))).astype(o_ref.dtype)


def _mem_attention(proj3, mkv3, tq):
    b, l, _ = proj3.shape
    cq, cz = COL_CQ // DC, COL_CZ // DC
    return pl.pallas_call(
        _mem_attn_kernel,
        grid=(b, HC, l // tq),
        in_specs=[pl.BlockSpec((1, tq, DC), lambda b_, h, i: (b_, i, cq + h)),
                  pl.BlockSpec((1, N_MEM, DC), lambda b_, h, i: (b_, 0, h)),
                  pl.BlockSpec((1, N_MEM, DC), lambda b_, h, i: (b_, 0, HC + h)),
                  pl.BlockSpec((1, tq, DC), lambda b_, h, i: (b_, i, cz + h))],
        out_specs=pl.BlockSpec((1, tq, DC), lambda b_, h, i: (b_, i, h)),
        out_shape=jax.ShapeDtypeStruct((b, l, HC * DC), BF16),
        compiler_params=_cparams(3),
        name="mem_attention",
    )(proj3, mkv3, mkv3, proj3)


def _merge_kernel(x_ref, ua_ref, ub_ref, uc_ref, ga_ref, gb_ref, gc_ref, wb_ref, wo_ref, pn_ref, o_ref):
    def branch(u_ref, g_ref, idx):
        y = jnp.dot(u_ref[...], wb_ref[idx], preferred_element_type=F32)
        return jax.nn.sigmoid(g_ref[...].astype(F32)) * y

    merged = branch(ua_ref, ga_ref, 0) + branch(ub_ref, gb_ref, 1) + branch(uc_ref, gc_ref, 2)
    y = jnp.dot(merged.astype(BF16), wo_ref[...], preferred_element_type=F32)
    ms = jnp.mean(y * y, axis=-1, keepdims=True)
    o_ref[...] = x_ref[...] + y * lax.rsqrt(ms + EPS) * pn_ref[...]


def _merge(x2d, ua, ub, uc, proj2, wb, wo, pn, tm):
    m, d = x2d.shape
    g0 = COL_G // d
    row = lambda: pl.BlockSpec((tm, d), lambda i: (i, 0))
    gate = lambda k: pl.BlockSpec((tm, d), lambda i: (i, g0 + k))
    return pl.pallas_call(
        _merge_kernel,
        grid=(m // tm,),
        in_specs=[row(), row(), row(), row(), gate(0), gate(1), gate(2),
                  pl.BlockSpec((3, d, d), lambda i: (0, 0, 0)),
                  pl.BlockSpec((d, d), lambda i: (0, 0)),
                  pl.BlockSpec((1, d), lambda i: (0, 0))],
        out_specs=row(),
        out_shape=jax.ShapeDtypeStruct((m, d), F32),
        compiler_params=_cparams(1),
        name="merge_out",
    )(x2d, ua, ub, uc, proj2, proj2, proj2, wb, wo, pn)


def _t5_bucket(rel):
    nb = NUM_BUCKETS // 2
    max_exact = nb // 2
    ret = jnp.where(rel > 0, nb, 0)
    n = jnp.abs(rel)
    nf = jnp.maximum(n, 1).astype(jnp.float32)
    large = max_exact + (jnp.log(nf / max_exact) / math.log(MAX_DISTANCE / max_exact)
                         * (nb - max_exact)).astype(jnp.int32)
    large = jnp.minimum(large, nb - 1)
    return ret + jnp.where(n < max_exact, n, large)


def _bias_blocks(rel_bias):
    n, w = BIAS_BLK, 2 * BIAS_BLK
    assert n >= MAX_DISTANCE
    k = jnp.arange(w, dtype=jnp.int32)
    k = jnp.where(k < n, k, k - w)
    o = jnp.arange(-(N_BIAS // 2), N_BIAS // 2 + 1, dtype=jnp.int32)
    u = rel_bias.astype(F32)[_t5_bucket(n * o[:, None] - k[None, :])] * LOG2E
    u = jnp.transpose(u, (2, 0, 1))
    flat = jnp.tile(u, (1, 1, n))[:, :, :n * (w - 1)]
    return flat.reshape(HA, N_BIAS, n, w - 1)[:, :, :, :n]


def _head_major(w):
    return w.reshape(D_MODEL, 2, HA, DA).transpose(0, 2, 1, 3).reshape(D_MODEL, 2 * HA * DA)


def kernel(x, mem, pre_norm, post_norm, w_in, lambda_q1, lambda_k1, lambda_q2, lambda_k2,
           diff_subln, rel_bias, lb_logits, hgrn_norm, mem_norm, w_mem_kv, w_branch, w_out):
    b, l, d = x.shape
    t = b * l
    layer = 0
    att_tq, att_tk = min(ATT_TQ, l), min(ATT_TK, l)

    lam_init = 0.8 - 0.6 * math.exp(-0.3 * layer)
    lam = (jnp.exp(jnp.sum(lambda_q1[layer].astype(F32) * lambda_k1[layer].astype(F32)))
           - jnp.exp(jnp.sum(lambda_q2[layer].astype(F32) * lambda_k2[layer].astype(F32)))
           + lam_init)
    assert att_tq % att_tk == 0 and att_tk % BIAS_BLK == 0
    bias_blk = _bias_blocks(rel_bias)
    saturated = bias_blk[:, (0, N_BIAS - 1), 0, 0]
    scalars = jnp.concatenate([lam.reshape(1), saturated.reshape(-1)]).astype(F32)
    lb_all = jnp.cumsum(jax.nn.softmax(lb_logits.astype(F32), axis=1), axis=1)
    lb2 = lb_all[:, layer, :]
    w_l = w_in[layer]
    w_in_p = jnp.concatenate([_head_major(w_l[:, COL_AQ:COL_AK]).astype(BF16),
                              _head_major(w_l[:, COL_AK:COL_AV]).astype(BF16),
                              w_l[:, COL_AV:].astype(BF16)], axis=1)
    blk = np.arange(HG_BLK)
    same = (blk[:, None] // CHUNK) == (blk[None, :] // CHUNK)
    tri_l = jnp.asarray(same & (blk[None, :] <= blk[:, None]), BF16)
    tri_u = jnp.asarray(same & (blk[None, :] >= blk[:, None]), BF16)

    proj2 = _norm_matmul(x.reshape(t, d), pre_norm[layer].reshape(1, d), w_in_p, tm=min(1024, t), tn=2048)
    proj3 = proj2.reshape(b, l, IN_COLS)

    ua = _diff_attention(proj3, scalars, bias_blk, diff_subln[layer].reshape(1, 2 * DA).astype(F32),
                         att_tq, att_tk, 1.0 - lam_init)

    ub = _hgrn(proj3, lb2, hgrn_norm[layer].reshape(1, VB).astype(F32), tri_l, tri_u)

    mkv = _norm_matmul(mem.reshape(b * N_MEM, d), mem_norm[layer].reshape(1, d),
                       w_mem_kv[layer].astype(BF16), tm=min(1024, b * N_MEM), tn=2048)
    uc = _mem_attention(proj3, mkv.reshape(b, N_MEM, 2 * HC * DC), tq=min(1024, l))

    out = _merge(x.reshape(t, d), ua.reshape(t, d), ub.reshape(t, d), uc.reshape(t, d), proj2,
                 w_branch[layer].astype(BF16), w_out[layer].astype(BF16),
                 post_norm[layer].reshape(1, d).astype(F32), tm=min(512, t))
    return out.reshape(b, l, d)
```

```python
import functools
import math

import jax
import jax.numpy as jnp
import numpy as np
from jax import lax
from jax.experimental import pallas as pl
from jax.experimental.pallas import tpu as pltpu

F32 = jnp.float32
BF16 = jnp.bfloat16

D_MODEL = 1024
N_MEM = 256
HA, DA = 8, 64
HB, KB, VB = 8, 128, 128
HC, DC = 4, 256
NUM_BUCKETS, MAX_DISTANCE = 32, 128
CHUNK = 64
EPS = 1e-6
LANE = 128

COL_AQ, COL_AK, COL_AV, COL_AZ = 0, 1024, 2048, 3072
COL_BQ, COL_BFF, COL_BFB, COL_BI, COL_BZ = 4096, 5120, 6144, 7168, 8192
COL_CQ, COL_CZ, COL_G = 9216, 10240, 11264
IN_COLS = 14336

VMEM_LIMIT = 56 * 1024 * 1024


def _cparams(n_axes, flags=None):
    return pltpu.CompilerParams(dimension_semantics=("arbitrary",) * n_axes,
                                vmem_limit_bytes=VMEM_LIMIT, flags=flags)


def _norm_matmul_kernel(x_ref, nw_ref, w_ref, o_ref, h_ref):
    @pl.when(pl.program_id(1) == 0)
    def _():
        x = x_ref[...]
        ms = jnp.mean(x * x, axis=-1, keepdims=True)
        h_ref[...] = (x * lax.rsqrt(ms + EPS) * nw_ref[...]).astype(BF16)

    o_ref[...] = jnp.dot(h_ref[...], w_ref[...], preferred_element_type=F32).astype(o_ref.dtype)


def _norm_matmul(x2d, norm_w, w_bf16, tm, tn):
    m, d = x2d.shape
    n = w_bf16.shape[1]
    return pl.pallas_call(
        _norm_matmul_kernel,
        grid=(m // tm, n // tn),
        in_specs=[pl.BlockSpec((tm, d), lambda i, j: (i, 0)),
                  pl.BlockSpec((1, d), lambda i, j: (0, 0)),
                  pl.BlockSpec((d, tn), lambda i, j: (0, j))],
        out_specs=pl.BlockSpec((tm, tn), lambda i, j: (i, j)),
        out_shape=jax.ShapeDtypeStruct((m, n), BF16),
        scratch_shapes=[pltpu.VMEM((tm, d), BF16)],
        compiler_params=_cparams(2),
        name="norm_matmul",
    )(x2d, norm_w, w_bf16)


ATT_GROUP = 256
ATT_TQ, ATT_TK = 1024, 512
BIAS_BLK = 128
N_BIAS = 5
LOG2E = math.log2(math.e)
SUM_ROWS = 16


def _attn_kernel(sc_ref, q1_ref, q2_ref, k1_ref, k2_ref, v_ref, z_ref, bias_ref, sub_ref, o_ref,
                 vt_ref, qqt_ref, acc_ref, m_ref, s_ref, mx_ref, c_ref, p_ref, a_ref,
                 *, tq, tk, nk, out_scale):
    h = pl.program_id(1)
    i = pl.program_id(2)
    lam = sc_ref[0]
    ng = 2 * tq // ATT_GROUP
    gpm = tq // ATT_GROUP
    kblks, qblks = tk // BIAS_BLK, ATT_GROUP // BIAS_BLK

    @pl.when(i == 0)
    def _():
        for c in range(nk):
            vt_ref[c, 0:LANE] = v_ref[0, c * tk:(c + 1) * tk, :].astype(F32).T.astype(BF16)
            vt_ref[c, LANE:LANE + SUM_ROWS] = jnp.ones((SUM_ROWS, tk), BF16)

    row = lax.broadcasted_iota(jnp.int32, (LANE, ATT_GROUP), 0)
    keep = (row // DA) == (h % (LANE // DA))
    for mp, q_ref in enumerate((q1_ref, q2_ref)):
        qt = (q_ref[0].astype(F32) * (DA ** -0.5 * LOG2E)).T
        for g in range(gpm):
            piece = qt[:, g * ATT_GROUP:(g + 1) * ATT_GROUP]
            qqt_ref[mp * gpm + g] = jnp.where(keep, piece, 0.0).astype(BF16)

    m_ref[...] = jnp.full(m_ref.shape, -jnp.inf, F32)
    acc_ref[...] = jnp.zeros(acc_ref.shape, F32)

    half = N_BIAS // 2
    c_left, c_right = sc_ref[1 + 2 * h], sc_ref[2 + 2 * h]
    ratio = (tq // BIAS_BLK) // kblks

    def stage_a(j, g, mode):
        qb0 = (g % gpm) * qblks
        mx = None
        for kb in range(kblks):
            rows = pl.ds(pl.multiple_of(j * tk + kb * BIAS_BLK, BIAS_BLK), BIAS_BLK)
            k_ref = k1_ref if g < gpm else k2_ref
            s = jnp.dot(k_ref[0, rows, :], qqt_ref[g], preferred_element_type=F32)
            cols, consts = [], []
            for qb in range(qblks):
                blk = s[:, qb * BIAS_BLK:(qb + 1) * BIAS_BLK]
                if mode == "table":
                    o = j * kblks + kb - (i * (tq // BIAS_BLK) + qb0 + qb)
                    blk, c = blk + bias_ref[0, jnp.clip(o, -half, half) + half], 0.0
                elif mode == "left":
                    c = c_left
                elif mode == "right":
                    c = c_right
                else:
                    o = mode * kblks + kb - (qb0 + qb)
                    if abs(o) < half:
                        blk, c = blk + bias_ref[0, o + half], 0.0
                    else:
                        c = c_left if o < 0 else c_right
                cols.append(blk)
                consts.append(jnp.full((1, BIAS_BLK), c, F32))
            s = jnp.concatenate(cols, axis=1)
            cvec = jnp.concatenate(consts, axis=1)
            s_ref[g % 2, kb * BIAS_BLK:(kb + 1) * BIAS_BLK] = s
            c_ref[g % 2, kb] = cvec
            cur = jnp.max(s, axis=0, keepdims=True) + cvec
            mx = cur if mx is None else jnp.maximum(mx, cur)
        mx_ref[g % 2] = mx

    def stage_b(j, g):
        m_prev = m_ref[g]
        m_new = jnp.maximum(m_prev, mx_ref[g % 2])
        m_ref[g] = m_new
        a_ref[g % 2] = jnp.exp2(m_prev - m_new)
        for kb in range(kblks):
            rows = slice(kb * BIAS_BLK, (kb + 1) * BIAS_BLK)
            p_ref[g % 2, rows] = jnp.exp2(s_ref[g % 2, rows] - (m_new - c_ref[g % 2, kb])).astype(BF16)

    def stage_c(j, g):
        acc_ref[g] = a_ref[g % 2] * acc_ref[g] + jnp.dot(vt_ref[j], p_ref[g % 2],
                                                         preferred_element_type=F32)

    def item_before(j, g, back):
        return (j, g - back) if g >= back else (j - 1, g - back + ng)

    def steps(j, mode, first=False):
        for g in range(ng):
            if not (first and g < 2):
                stage_c(*item_before(j, g, 2))
            if not (first and g < 1):
                stage_b(*item_before(j, g, 1))
            stage_a(j, g, mode)

    def tiles(lo, hi, mode):
        def body(j, carry):
            steps(j, mode)
            return carry
        lax.fori_loop(lo, hi, body, 0)

    steps(0, "table", first=True)
    tiles(1, jnp.maximum(ratio * i - 1, 1), "left")
    for w in range(-1, ratio + 1):
        j_w = ratio * i + w

        @pl.when((j_w >= 1) & (j_w < nk))
        def _():
            steps(j_w, w)
    tiles(jnp.maximum(ratio * i + ratio + 1, 1), nk, "right")
    stage_c(nk - 1, ng - 2)
    stage_b(nk - 1, ng - 1)
    stage_c(nk - 1, ng - 1)

    def normalized(g):
        acc = acc_ref[g]
        return acc[0:LANE] / acc[LANE:LANE + 1]

    d = jnp.concatenate([normalized(g) - lam * normalized(g + gpm) for g in range(gpm)],
                        axis=1).T
    ms = jnp.mean(d * d, axis=-1, keepdims=True)
    y = d * lax.rsqrt(ms + EPS) * sub_ref[...] * out_scale
    z = z_ref[0].astype(F32)
    o_ref[0] = (y * (z * jax.nn.sigmoid(z))).astype(o_ref.dtype)


def _diff_attention(proj3, scalars, bias_blk, subln, tq, tk, out_scale):
    b, l, _ = proj3.shape
    nk = l // tk
    ng = 2 * tq // ATT_GROUP
    cq, ck, cv, cz = COL_AQ // LANE, COL_AK // LANE, COL_AV // LANE, COL_AZ // LANE
    hpb = LANE // DA
    mapw = HA * DA // LANE
    kern = functools.partial(_attn_kernel, tq=tq, tk=tk, nk=nk, out_scale=out_scale)
    return pl.pallas_call(
        kern,
        grid=(b, HA, l // tq),
        in_specs=[pl.BlockSpec(memory_space=pltpu.SMEM),
                  pl.BlockSpec((1, tq, LANE), lambda b_, h, i: (b_, i, cq + h // hpb)),
                  pl.BlockSpec((1, tq, LANE), lambda b_, h, i: (b_, i, cq + mapw + h // hpb)),
                  pl.BlockSpec((1, l, LANE), lambda b_, h, i: (b_, 0, ck + h // hpb)),
                  pl.BlockSpec((1, l, LANE), lambda b_, h, i: (b_, 0, ck + mapw + h // hpb)),
                  pl.BlockSpec((1, l, LANE), lambda b_, h, i: (b_, 0, cv + h)),
                  pl.BlockSpec((1, tq, LANE), lambda b_, h, i: (b_, i, cz + h)),
                  pl.BlockSpec((1, N_BIAS, BIAS_BLK, BIAS_BLK), lambda b_, h, i: (h, 0, 0, 0)),
                  pl.BlockSpec((1, LANE), lambda b_, h, i: (0, 0))],
        out_specs=pl.BlockSpec((1, tq, LANE), lambda b_, h, i: (b_, i, h)),
        out_shape=jax.ShapeDtypeStruct((b, l, HA * 2 * DA), BF16),
        scratch_shapes=[pltpu.VMEM((nk, LANE + SUM_ROWS, tk), BF16),
                        pltpu.VMEM((ng, LANE, ATT_GROUP), BF16),
                        pltpu.VMEM((ng, LANE + SUM_ROWS, ATT_GROUP), F32),
                        pltpu.VMEM((ng, 1, ATT_GROUP), F32),
                        pltpu.VMEM((2, tk, ATT_GROUP), F32),
                        pltpu.VMEM((2, 1, ATT_GROUP), F32),
                        pltpu.VMEM((2, tk // BIAS_BLK, 1, ATT_GROUP), F32),
                        pltpu.VMEM((2, tk, ATT_GROUP), BF16),
                        pltpu.VMEM((2, 1, ATT_GROUP), F32)],
        compiler_params=_cparams(3),
        name="diff_attention",
    )(scalars, proj3, proj3, proj3, proj3, proj3, proj3, bias_blk, subln)


HG_BLK = 256


def _hgrn_kernel(q_ref, ff_ref, fb_ref, v_ref, z_ref, lb_ref, nw_ref, tl_ref, tu_ref, ml_ref, mu_ref, o_ref,
                 qtf_ref, ktf_ref, qtb_ref, ktb_ref, qh_ref, st_ref, sf_ref, sb_ref,
                 kh_ref, dec_ref, a_ref, *, l):
    nblk = l // HG_BLK
    cpb = HG_BLK // CHUNK
    tn = (((0,), (0,)), ((), ()))
    nt = (((1,), (1,)), ((), ()))
    dirs = ((ff_ref, 0, tl_ref, CHUNK // 2 - 1, CHUNK - 1, qtf_ref, ktf_ref, sf_ref),
            (fb_ref, 1, tu_ref, CHUNK // 2, 0, qtb_ref, ktb_ref, sb_ref))

    def block_of(r, di):
        return r if di == 0 else nblk - 1 - r

    def cumsum(tri, g):
        g_hi = g.astype(BF16)
        g_lo = (g - g_hi.astype(F32)).astype(BF16)
        return (jnp.dot(tri, g_hi, preferred_element_type=F32)
                + jnp.dot(tri, g_lo, preferred_element_type=F32))

    def prep(r):
        for f_ref, di, tri_ref, mid_row, last_row, qt_ref, kt_ref, _ in dirs:
            rows = pl.ds(pl.multiple_of(block_of(r, di) * HG_BLK, HG_BLK), HG_BLK)
            lb = lb_ref[di:di + 1, :]
            x = f_ref[0, rows, :].astype(F32)
            q3 = q_ref[0, rows, :].astype(F32).reshape(cpb, CHUNK, LANE)
            f = lb + (1.0 - lb) * jax.nn.sigmoid(x)
            g = jnp.log(f)
            k3 = (1.0 - f).reshape(cpb, CHUNK, LANE)
            bcs = cumsum(tri_ref[...], g).reshape(cpb, CHUNK, LANE)
            b_mid = bcs[:, mid_row:mid_row + 1, :]
            b_last = bcs[:, last_row:last_row + 1, :]
            qt = q3 * jnp.exp(bcs - b_mid)
            kt = k3 * jnp.exp(b_mid - bcs)
            qh = qt * jnp.exp(b_mid)
            kh = kt * jnp.exp(b_last - b_mid)
            qt_ref[rows, :] = qt.reshape(HG_BLK, LANE).astype(BF16)
            kt_ref[rows, :] = kt.reshape(HG_BLK, LANE).astype(BF16)
            qh_ref[rows, di * KB:(di + 1) * KB] = qh.reshape(HG_BLK, LANE).astype(BF16)
            kh_ref[r % 2, di] = kh.reshape(HG_BLK, LANE).astype(BF16)
            dec_ref[r % 2, di] = jnp.exp(b_last)

    def scan(r):
        for _, di, _, _, _, _, _, s_ref in dirs:
            blk = block_of(r, di)
            s = s_ref[...]
            for c in (range(cpb) if di == 0 else reversed(range(cpb))):
                st_ref[blk * cpb + c, :, di * KB:(di + 1) * KB] = s.astype(BF16)
                vrows = pl.ds(pl.multiple_of(blk * HG_BLK + c * CHUNK, CHUNK), CHUNK)
                u = lax.dot_general(v_ref[0, vrows, :], kh_ref[r % 2, di, c * CHUNK:(c + 1) * CHUNK, :], tn,
                                    preferred_element_type=F32)
                s = s * dec_ref[r % 2, di, c] + u
            s_ref[...] = s

    sf_ref[...] = jnp.zeros(sf_ref.shape, F32)
    sb_ref[...] = jnp.zeros(sb_ref.shape, F32)
    prep(0)

    def pass_a(r, carry):
        scan(r - 1)
        prep(r)
        return carry

    lax.fori_loop(1, nblk, pass_a, 0)
    scan(nblk - 1)

    nw = nw_ref[...]

    def scores(r):
        rows = pl.ds(pl.multiple_of(r * HG_BLK, HG_BLK), HG_BLK)
        a_f = lax.dot_general(qtf_ref[rows, :], ktf_ref[rows, :], nt, preferred_element_type=F32)
        a_b = lax.dot_general(qtb_ref[rows, :], ktb_ref[rows, :], nt, preferred_element_type=F32)
        a = jnp.where(ml_ref[...] != 0.0, a_f, 0.0) + jnp.where(mu_ref[...] != 0.0, a_b, 0.0)
        a_ref[r % 2] = a.astype(BF16)

    def outputs(r):
        rows = pl.ds(pl.multiple_of(r * HG_BLK, HG_BLK), HG_BLK)
        inter = jnp.concatenate(
            [lax.dot_general(qh_ref[pl.ds(pl.multiple_of(r * HG_BLK + c * CHUNK, CHUNK), CHUNK), :],
                             st_ref[r * cpb + c], nt, preferred_element_type=F32) for c in range(cpb)], axis=0)
        o = jnp.dot(a_ref[r % 2], v_ref[0, rows, :], preferred_element_type=F32) + inter
        ms = jnp.mean(o * o, axis=-1, keepdims=True)
        y = o * lax.rsqrt(ms + EPS) * nw
        z = z_ref[0, rows, :].astype(F32)
        o_ref[0, rows, :] = (y * (z * jax.nn.sigmoid(z))).astype(o_ref.dtype)

    scores(0)

    def pass_b(r, carry):
        outputs(r - 1)
        scores(r)
        return carry

    lax.fori_loop(1, nblk, pass_b, 0)
    outputs(nblk - 1)


def _hgrn(proj3, lb2, hnorm, tri_l, tri_u):
    b, l, _ = proj3.shape
    n = l // CHUNK
    cpb = HG_BLK // CHUNK
    c = lambda col: col // LANE
    seq = lambda col: pl.BlockSpec((1, l, LANE), lambda b_, h: (b_, 0, c(col) + h))
    blk = lambda: pl.BlockSpec((HG_BLK, HG_BLK), lambda b_, h: (0, 0))
    return pl.pallas_call(
        functools.partial(_hgrn_kernel, l=l),
        grid=(b, HB),
        in_specs=[seq(COL_BQ), seq(COL_BFF), seq(COL_BFB), seq(COL_BI), seq(COL_BZ),
                  pl.BlockSpec((2, LANE), lambda b_, h: (0, h)),
                  pl.BlockSpec((1, LANE), lambda b_, h: (0, 0)),
                  blk(), blk(), blk(), blk()],
        out_specs=pl.BlockSpec((1, l, LANE), lambda b_, h: (b_, 0, h)),
        out_shape=jax.ShapeDtypeStruct((b, l, HB * VB), BF16),
        scratch_shapes=[pltpu.VMEM((l, KB), BF16)] * 4
                       + [pltpu.VMEM((l, 2 * KB), BF16),
                          pltpu.VMEM((n, VB, 2 * KB), BF16),
                          pltpu.VMEM((VB, KB), F32), pltpu.VMEM((VB, KB), F32),
                          pltpu.VMEM((2, 2, HG_BLK, KB), BF16),
                          pltpu.VMEM((2, 2, cpb, 1, KB), F32),
                          pltpu.VMEM((2, HG_BLK, HG_BLK), BF16)],
        compiler_params=_cparams(2),
        name="hgrn2",
    )(proj3, proj3, proj3, proj3, proj3, lb2, hnorm, tri_l, tri_u,
      tri_l.astype(F32), tri_u.astype(F32))


def _merge_kernel(x_ref, ua_ref, ub_ref, cq_ref, cz_ref, mkv_ref, ga_ref, gb_ref, gc_ref,
                  wb_ref, wo_ref, pn_ref, o_ref):
    def mem_head(c):
        cols = slice(c * DC, (c + 1) * DC)
        q = cq_ref[:, cols] * jnp.asarray(DC ** -0.5, BF16)
        s = lax.dot_general(q, mkv_ref[0, :, cols], (((1,), (1,)), ((), ())), preferred_element_type=F32)
        e = jnp.exp(s - jnp.max(s, axis=-1, keepdims=True))
        p = e / jnp.sum(e, axis=-1, keepdims=True)
        o = jnp.dot(p.astype(BF16), mkv_ref[0, :, HC * DC + c * DC:HC * DC + (c + 1) * DC],
                    preferred_element_type=F32)
        z = cz_ref[:, cols].astype(F32)
        return (o * (z * jax.nn.sigmoid(z))).astype(BF16)

    def branch(u, g_ref, idx):
        y = jnp.dot(u, wb_ref[idx], preferred_element_type=F32)
        return jax.nn.sigmoid(g_ref[...].astype(F32)) * y

    uc = jnp.concatenate([mem_head(c) for c in range(HC)], axis=1)
    merged = branch(ua_ref[...], ga_ref, 0) + branch(ub_ref[...], gb_ref, 1) + branch(uc, gc_ref, 2)
    y = jnp.dot(merged.astype(BF16), wo_ref[...], preferred_element_type=F32)
    ms = jnp.mean(y * y, axis=-1, keepdims=True)
    o_ref[...] = x_ref[...] + y * lax.rsqrt(ms + EPS) * pn_ref[...]


def _merge(x2d, ua, ub, proj2, mkv3, wb, wo, pn, tm, rows_per_batch):
    m, d = x2d.shape
    col = lambda c: pl.BlockSpec((tm, d), lambda i: (i, c // d))
    tiles_per_batch = rows_per_batch // tm
    return pl.pallas_call(
        _merge_kernel,
        grid=(m // tm,),
        in_specs=[col(0), col(0), col(0), col(COL_CQ), col(COL_CZ),
                  pl.BlockSpec((1, N_MEM, 2 * HC * DC), lambda i: (i // tiles_per_batch, 0, 0)),
                  col(COL_G), col(COL_G + d), col(COL_G + 2 * d),
                  pl.BlockSpec((3, d, d), lambda i: (0, 0, 0)),
                  pl.BlockSpec((d, d), lambda i: (0, 0)),
                  pl.BlockSpec((1, d), lambda i: (0, 0))],
        out_specs=col(0),
        out_shape=jax.ShapeDtypeStruct((m, d), F32),
        compiler_params=_cparams(1),
        name="merge_out",
    )(x2d, ua, ub, proj2, proj2, mkv3, proj2, proj2, proj2, wb, wo, pn)


def _t5_bucket(rel):
    nb = NUM_BUCKETS // 2
    max_exact = nb // 2
    ret = jnp.where(rel > 0, nb, 0)
    n = jnp.abs(rel)
    nf = jnp.maximum(n, 1).astype(jnp.float32)
    large = max_exact + (jnp.log(nf / max_exact) / math.log(MAX_DISTANCE / max_exact)
                         * (nb - max_exact)).astype(jnp.int32)
    large = jnp.minimum(large, nb - 1)
    return ret + jnp.where(n < max_exact, n, large)


def _bias_blocks(rel_bias):
    n, w = BIAS_BLK, 2 * BIAS_BLK
    assert n >= MAX_DISTANCE
    k = jnp.arange(w, dtype=jnp.int32)
    k = jnp.where(k < n, k, k - w)
    o = jnp.arange(-(N_BIAS // 2), N_BIAS // 2 + 1, dtype=jnp.int32)
    u = rel_bias.astype(F32)[_t5_bucket(n * o[:, None] - k[None, :])] * LOG2E
    u = jnp.transpose(u, (2, 0, 1))
    flat = jnp.tile(u, (1, 1, n))[:, :, :n * (w - 1)]
    return flat.reshape(HA, N_BIAS, n, w - 1)[:, :, :, :n]


def kernel(x, mem, pre_norm, post_norm, w_in, lambda_q1, lambda_k1, lambda_q2, lambda_k2,
           diff_subln, rel_bias, lb_logits, hgrn_norm, mem_norm, w_mem_kv, w_branch, w_out):
    b, l, d = x.shape
    t = b * l
    layer = 0
    att_tq, att_tk = min(ATT_TQ, l), min(ATT_TK, l)

    lam_init = 0.8 - 0.6 * math.exp(-0.3 * layer)
    lam = (jnp.exp(jnp.sum(lambda_q1[layer].astype(F32) * lambda_k1[layer].astype(F32)))
           - jnp.exp(jnp.sum(lambda_q2[layer].astype(F32) * lambda_k2[layer].astype(F32)))
           + lam_init)
    assert att_tq % att_tk == 0 and att_tk % BIAS_BLK == 0
    bias_blk = _bias_blocks(rel_bias)
    saturated = bias_blk[:, (0, N_BIAS - 1), 0, 0]
    scalars = jnp.concatenate([lam.reshape(1), saturated.reshape(-1)]).astype(F32)
    lb_all = jnp.cumsum(jax.nn.softmax(lb_logits.astype(F32), axis=1), axis=1)
    lb2 = lb_all[:, layer, :]
    w_in_p = w_in[layer].astype(BF16)
    blk = np.arange(HG_BLK)
    same = (blk[:, None] // CHUNK) == (blk[None, :] // CHUNK)
    tri_l = jnp.asarray(same & (blk[None, :] <= blk[:, None]), BF16)
    tri_u = jnp.asarray(same & (blk[None, :] >= blk[:, None]), BF16)

    proj2 = _norm_matmul(x.reshape(t, d), pre_norm[layer].reshape(1, d), w_in_p, tm=min(1024, t), tn=2048)
    proj3 = proj2.reshape(b, l, IN_COLS)

    ua = _diff_attention(proj3, scalars, bias_blk, diff_subln[layer].reshape(1, 2 * DA).astype(F32),
                         att_tq, att_tk, 1.0 - lam_init)

    ub = _hgrn(proj3, lb2, hgrn_norm[layer].reshape(1, VB).astype(F32), tri_l, tri_u)

    mkv = _norm_matmul(mem.reshape(b * N_MEM, d), mem_norm[layer].reshape(1, d),
                       w_mem_kv[layer].astype(BF16), tm=min(1024, b * N_MEM), tn=2048)
    tm = min(512, l)
    out = _merge(x.reshape(t, d), ua.reshape(t, d), ub.reshape(t, d), proj2,
                 mkv.reshape(b, N_MEM, 2 * HC * DC), w_branch[layer].astype(BF16), w_out[layer].astype(BF16),
                 post_norm[layer].reshape(1, d).astype(F32), tm=tm, rows_per_batch=l)
    return out.reshape(b, l, d)
```

```python
import functools
import math

import jax
import jax.numpy as jnp
import numpy as np
from jax import lax
from jax.experimental import pallas as pl
from jax.experimental.pallas import tpu as pltpu

F32 = jnp.float32
BF16 = jnp.bfloat16

D_MODEL = 1024
N_MEM = 256
HA, DA = 8, 64
HB, KB, VB = 8, 128, 128
HC, DC = 4, 256
NUM_BUCKETS, MAX_DISTANCE = 32, 128
CHUNK = 64
EPS = 1e-6
LANE = 128

COL_AQ, COL_AK, COL_AV, COL_AZ = 0, 1024, 2048, 3072
COL_BQ, COL_BFF, COL_BFB, COL_BI, COL_BZ = 4096, 5120, 6144, 7168, 8192
COL_CQ, COL_CZ, COL_G = 9216, 10240, 11264
IN_COLS = 14336

VMEM_LIMIT = 56 * 1024 * 1024


def _cparams(n_axes, flags=None):
    return pltpu.CompilerParams(dimension_semantics=("arbitrary",) * n_axes,
                                vmem_limit_bytes=VMEM_LIMIT, flags=flags)


def _norm_matmul_kernel(x_ref, nw_ref, w_ref, o_ref, h_ref):
    @pl.when(pl.program_id(1) == 0)
    def _():
        x = x_ref[...]
        ms = jnp.mean(x * x, axis=-1, keepdims=True)
        h_ref[...] = (x * lax.rsqrt(ms + EPS) * nw_ref[...]).astype(BF16)

    o_ref[...] = jnp.dot(h_ref[...], w_ref[...], preferred_element_type=F32).astype(o_ref.dtype)


def _norm_matmul(x2d, norm_w, w_bf16, tm, tn):
    m, d = x2d.shape
    n = w_bf16.shape[1]
    return pl.pallas_call(
        _norm_matmul_kernel,
        grid=(m // tm, n // tn),
        in_specs=[pl.BlockSpec((tm, d), lambda i, j: (i, 0)),
                  pl.BlockSpec((1, d), lambda i, j: (0, 0)),
                  pl.BlockSpec((d, tn), lambda i, j: (0, j))],
        out_specs=pl.BlockSpec((tm, tn), lambda i, j: (i, j)),
        out_shape=jax.ShapeDtypeStruct((m, n), BF16),
        scratch_shapes=[pltpu.VMEM((tm, d), BF16)],
        compiler_params=_cparams(2),
        name="norm_matmul",
    )(x2d, norm_w, w_bf16)


ATT_GROUP = 256
ATT_TQ, ATT_TK = 1024, 512
BIAS_BLK = 128
N_BIAS = 5
LOG2E = math.log2(math.e)
SUM_ROWS = 16


def _attn_kernel(sc_ref, q1_ref, q2_ref, k1_ref, k2_ref, v_ref, z_ref, bias_ref, sub_ref, o_ref,
                 vt_ref, qqt_ref, acc_ref, m_ref, s_ref, mx_ref, c_ref, p_ref, a_ref,
                 *, tq, tk, nk, out_scale):
    h = pl.program_id(1)
    i = pl.program_id(2)
    lam = sc_ref[0]
    ng = 2 * tq // ATT_GROUP
    gpm = tq // ATT_GROUP
    kblks, qblks = tk // BIAS_BLK, ATT_GROUP // BIAS_BLK

    @pl.when(i == 0)
    def _():
        for c in range(nk):
            vt_ref[c, 0:LANE] = v_ref[0, c * tk:(c + 1) * tk, :].astype(F32).T.astype(BF16)
            vt_ref[c, LANE:LANE + SUM_ROWS] = jnp.ones((SUM_ROWS, tk), BF16)

    row = lax.broadcasted_iota(jnp.int32, (LANE, ATT_GROUP), 0)
    keep = (row // DA) == (h % (LANE // DA))
    for mp, q_ref in enumerate((q1_ref, q2_ref)):
        qt = (q_ref[0].astype(F32) * (DA ** -0.5 * LOG2E)).T
        for g in range(gpm):
            piece = qt[:, g * ATT_GROUP:(g + 1) * ATT_GROUP]
            qqt_ref[mp * gpm + g] = jnp.where(keep, piece, 0.0).astype(BF16)

    m_ref[...] = jnp.full(m_ref.shape, -jnp.inf, F32)
    acc_ref[...] = jnp.zeros(acc_ref.shape, F32)

    half = N_BIAS // 2
    c_left, c_right = sc_ref[1 + 2 * h], sc_ref[2 + 2 * h]
    ratio = (tq // BIAS_BLK) // kblks

    def stage_a(j, g, mode):
        qb0 = (g % gpm) * qblks
        mx = None
        for kb in range(kblks):
            rows = pl.ds(pl.multiple_of(j * tk + kb * BIAS_BLK, BIAS_BLK), BIAS_BLK)
            k_ref = k1_ref if g < gpm else k2_ref
            s = jnp.dot(k_ref[0, rows, :], qqt_ref[g], preferred_element_type=F32)
            cols, consts = [], []
            for qb in range(qblks):
                blk = s[:, qb * BIAS_BLK:(qb + 1) * BIAS_BLK]
                if mode == "table":
                    o = j * kblks + kb - (i * (tq // BIAS_BLK) + qb0 + qb)
                    blk, c = blk + bias_ref[0, jnp.clip(o, -half, half) + half], 0.0
                elif mode == "left":
                    c = c_left
                elif mode == "right":
                    c = c_right
                else:
                    o = mode * kblks + kb - (qb0 + qb)
                    if abs(o) < half:
                        blk, c = blk + bias_ref[0, o + half], 0.0
                    else:
                        c = c_left if o < 0 else c_right
                cols.append(blk)
                consts.append(jnp.full((1, BIAS_BLK), c, F32))
            s = jnp.concatenate(cols, axis=1)
            cvec = jnp.concatenate(consts, axis=1)
            s_ref[g % 2, kb * BIAS_BLK:(kb + 1) * BIAS_BLK] = s
            c_ref[g % 2, kb] = cvec
            cur = jnp.max(s, axis=0, keepdims=True) + cvec
            mx = cur if mx is None else jnp.maximum(mx, cur)
        mx_ref[g % 2] = mx

    def stage_b(j, g):
        m_prev = m_ref[g]
        m_new = jnp.maximum(m_prev, mx_ref[g % 2])
        m_ref[g] = m_new
        a_ref[g % 2] = jnp.exp2(m_prev - m_new)
        for kb in range(kblks):
            rows = slice(kb * BIAS_BLK, (kb + 1) * BIAS_BLK)
            p_ref[g % 2, rows] = jnp.exp2(s_ref[g % 2, rows] - (m_new - c_ref[g % 2, kb])).astype(BF16)

    def stage_c(j, g):
        acc_ref[g] = a_ref[g % 2] * acc_ref[g] + jnp.dot(vt_ref[j], p_ref[g % 2],
                                                         preferred_element_type=F32)

    def item_before(j, g, back):
        return (j, g - back) if g >= back else (j - 1, g - back + ng)

    def steps(j, mode, first=False):
        for g in range(ng):
            if not (first and g < 2):
                stage_c(*item_before(j, g, 2))
            if not (first and g < 1):
                stage_b(*item_before(j, g, 1))
            stage_a(j, g, mode)

    def tiles(lo, hi, mode):
        def body(j, carry):
            steps(j, mode)
            return carry
        lax.fori_loop(lo, hi, body, 0)

    steps(0, "table", first=True)
    tiles(1, jnp.maximum(ratio * i - 1, 1), "left")
    for w in range(-1, ratio + 1):
        j_w = ratio * i + w

        @pl.when((j_w >= 1) & (j_w < nk))
        def _():
            steps(j_w, w)
    tiles(jnp.maximum(ratio * i + ratio + 1, 1), nk, "right")
    stage_c(nk - 1, ng - 2)
    stage_b(nk - 1, ng - 1)
    stage_c(nk - 1, ng - 1)

    def normalized(g):
        acc = acc_ref[g]
        return acc[0:LANE] / acc[LANE:LANE + 1]

    d = jnp.concatenate([normalized(g) - lam * normalized(g + gpm) for g in range(gpm)],
                        axis=1).T
    ms = jnp.mean(d * d, axis=-1, keepdims=True)
    y = d * lax.rsqrt(ms + EPS) * sub_ref[...] * out_scale
    z = z_ref[0].astype(F32)
    o_ref[0] = (y * (z * jax.nn.sigmoid(z))).astype(o_ref.dtype)


def _diff_attention(proj3, scalars, bias_blk, subln, tq, tk, out_scale):
    b, l, _ = proj3.shape
    nk = l // tk
    ng = 2 * tq // ATT_GROUP
    cq, ck, cv, cz = COL_AQ // LANE, COL_AK // LANE, COL_AV // LANE, COL_AZ // LANE
    hpb = LANE // DA
    mapw = HA * DA // LANE
    kern = functools.partial(_attn_kernel, tq=tq, tk=tk, nk=nk, out_scale=out_scale)
    return pl.pallas_call(
        kern,
        grid=(b, HA, l // tq),
        in_specs=[pl.BlockSpec(memory_space=pltpu.SMEM),
                  pl.BlockSpec((1, tq, LANE), lambda b_, h, i: (b_, i, cq + h // hpb)),
                  pl.BlockSpec((1, tq, LANE), lambda b_, h, i: (b_, i, cq + mapw + h // hpb)),
                  pl.BlockSpec((1, l, LANE), lambda b_, h, i: (b_, 0, ck + h // hpb)),
                  pl.BlockSpec((1, l, LANE), lambda b_, h, i: (b_, 0, ck + mapw + h // hpb)),
                  pl.BlockSpec((1, l, LANE), lambda b_, h, i: (b_, 0, cv + h)),
                  pl.BlockSpec((1, tq, LANE), lambda b_, h, i: (b_, i, cz + h)),
                  pl.BlockSpec((1, N_BIAS, BIAS_BLK, BIAS_BLK), lambda b_, h, i: (h, 0, 0, 0)),
                  pl.BlockSpec((1, LANE), lambda b_, h, i: (0, 0))],
        out_specs=pl.BlockSpec((1, tq, LANE), lambda b_, h, i: (b_, i, h)),
        out_shape=jax.ShapeDtypeStruct((b, l, HA * 2 * DA), BF16),
        scratch_shapes=[pltpu.VMEM((nk, LANE + SUM_ROWS, tk), BF16),
                        pltpu.VMEM((ng, LANE, ATT_GROUP), BF16),
                        pltpu.VMEM((ng, LANE + SUM_ROWS, ATT_GROUP), F32),
                        pltpu.VMEM((ng, 1, ATT_GROUP), F32),
                        pltpu.VMEM((2, tk, ATT_GROUP), F32),
                        pltpu.VMEM((2, 1, ATT_GROUP), F32),
                        pltpu.VMEM((2, tk // BIAS_BLK, 1, ATT_GROUP), F32),
                        pltpu.VMEM((2, tk, ATT_GROUP), BF16),
                        pltpu.VMEM((2, 1, ATT_GROUP), F32)],
        compiler_params=_cparams(3),
        name="diff_attention",
    )(scalars, proj3, proj3, proj3, proj3, proj3, proj3, bias_blk, subln)


HG_BLK = 256
A_UNROLL, B_UNROLL = 2, 4


def _hgrn_kernel(q_ref, ff_ref, fb_ref, v_ref, z_ref, lb_ref, nw_ref, tl_ref, tu_ref, ml_ref, mu_ref, o_ref,
                 qtf_ref, ktf_ref, qtb_ref, ktb_ref, qh_ref, st_ref, sf_ref, sb_ref,
                 kh_ref, dec_ref, a_ref, *, l):
    nblk = l // HG_BLK
    cpb = HG_BLK // CHUNK
    tn = (((0,), (0,)), ((), ()))
    nt = (((1,), (1,)), ((), ()))
    dirs = ((ff_ref, 0, tl_ref, CHUNK // 2 - 1, CHUNK - 1, qtf_ref, ktf_ref, sf_ref),
            (fb_ref, 1, tu_ref, CHUNK // 2, 0, qtb_ref, ktb_ref, sb_ref))

    def block_of(r, di):
        return r if di == 0 else nblk - 1 - r

    def cumsum(tri, g):
        g_hi = g.astype(BF16)
        g_lo = (g - g_hi.astype(F32)).astype(BF16)
        return (jnp.dot(tri, g_hi, preferred_element_type=F32)
                + jnp.dot(tri, g_lo, preferred_element_type=F32))

    def prep(r):
        for f_ref, di, tri_ref, mid_row, last_row, qt_ref, kt_ref, _ in dirs:
            rows = pl.ds(pl.multiple_of(block_of(r, di) * HG_BLK, HG_BLK), HG_BLK)
            lb = lb_ref[di:di + 1, :]
            x = f_ref[0, rows, :].astype(F32)
            q3 = q_ref[0, rows, :].astype(F32).reshape(cpb, CHUNK, LANE)
            f = lb + (1.0 - lb) * jax.nn.sigmoid(x)
            g = jnp.log(f)
            k3 = (1.0 - f).reshape(cpb, CHUNK, LANE)
            bcs = cumsum(tri_ref[...], g).reshape(cpb, CHUNK, LANE)
            b_mid = bcs[:, mid_row:mid_row + 1, :]
            b_last = bcs[:, last_row:last_row + 1, :]
            qt = q3 * jnp.exp(bcs - b_mid)
            kt = k3 * jnp.exp(b_mid - bcs)
            qh = qt * jnp.exp(b_mid)
            kh = kt * jnp.exp(b_last - b_mid)
            qt_ref[rows, :] = qt.reshape(HG_BLK, LANE).astype(BF16)
            kt_ref[rows, :] = kt.reshape(HG_BLK, LANE).astype(BF16)
            qh_ref[rows, di * KB:(di + 1) * KB] = qh.reshape(HG_BLK, LANE).astype(BF16)
            kh_ref[r % (2 * A_UNROLL), di] = kh.reshape(HG_BLK, LANE).astype(BF16)
            dec_ref[r % (2 * A_UNROLL), di] = jnp.exp(b_last)

    def scan(r):
        for _, di, _, _, _, _, _, s_ref in dirs:
            blk = block_of(r, di)
            s = s_ref[...]
            for c in (range(cpb) if di == 0 else reversed(range(cpb))):
                st_ref[blk * cpb + c, :, di * KB:(di + 1) * KB] = s.astype(BF16)
                vrows = pl.ds(pl.multiple_of(blk * HG_BLK + c * CHUNK, CHUNK), CHUNK)
                u = lax.dot_general(v_ref[0, vrows, :],
                                    kh_ref[r % (2 * A_UNROLL), di, c * CHUNK:(c + 1) * CHUNK, :], tn,
                                    preferred_element_type=F32)
                s = s * dec_ref[r % (2 * A_UNROLL), di, c] + u
            s_ref[...] = s

    def pipelined(first, second, unroll):
        def group(fn, gi):
            for u in range(unroll):
                fn(gi * unroll + u)

        group(first, 0)

        def body(gi, carry):
            group(second, gi - 1)
            group(first, gi)
            return carry

        lax.fori_loop(1, nblk // unroll, body, 0)
        group(second, nblk // unroll - 1)

    sf_ref[...] = jnp.zeros(sf_ref.shape, F32)
    sb_ref[...] = jnp.zeros(sb_ref.shape, F32)
    pipelined(prep, scan, A_UNROLL)

    nw = nw_ref[...]

    def scores(r):
        rows = pl.ds(pl.multiple_of(r * HG_BLK, HG_BLK), HG_BLK)
        a_f = lax.dot_general(qtf_ref[rows, :], ktf_ref[rows, :], nt, preferred_element_type=F32)
        a_b = lax.dot_general(qtb_ref[rows, :], ktb_ref[rows, :], nt, preferred_element_type=F32)
        a = jnp.where(ml_ref[...] != 0.0, a_f, 0.0) + jnp.where(mu_ref[...] != 0.0, a_b, 0.0)
        a_ref[r % (2 * B_UNROLL)] = a.astype(BF16)

    def outputs(r):
        rows = pl.ds(pl.multiple_of(r * HG_BLK, HG_BLK), HG_BLK)
        inter = jnp.concatenate(
            [lax.dot_general(qh_ref[pl.ds(pl.multiple_of(r * HG_BLK + c * CHUNK, CHUNK), CHUNK), :],
                             st_ref[r * cpb + c], nt, preferred_element_type=F32) for c in range(cpb)], axis=0)
        o = jnp.dot(a_ref[r % (2 * B_UNROLL)], v_ref[0, rows, :], preferred_element_type=F32) + inter
        ms = jnp.mean(o * o, axis=-1, keepdims=True)
        y = o * lax.rsqrt(ms + EPS) * nw
        z = z_ref[0, rows, :].astype(F32)
        o_ref[0, rows, :] = (y * (z * jax.nn.sigmoid(z))).astype(o_ref.dtype)

    pipelined(scores, outputs, B_UNROLL)


def _hgrn(proj3, lb2, hnorm, tri_l, tri_u):
    b, l, _ = proj3.shape
    n = l // CHUNK
    cpb = HG_BLK // CHUNK
    c = lambda col: col // LANE
    seq = lambda col: pl.BlockSpec((1, l, LANE), lambda b_, h: (b_, 0, c(col) + h))
    blk = lambda: pl.BlockSpec((HG_BLK, HG_BLK), lambda b_, h: (0, 0))
    return pl.pallas_call(
        functools.partial(_hgrn_kernel, l=l),
        grid=(b, HB),
        in_specs=[seq(COL_BQ), seq(COL_BFF), seq(COL_BFB), seq(COL_BI), seq(COL_BZ),
                  pl.BlockSpec((2, LANE), lambda b_, h: (0, h)),
                  pl.BlockSpec((1, LANE), lambda b_, h: (0, 0)),
                  blk(), blk(), blk(), blk()],
        out_specs=pl.BlockSpec((1, l, LANE), lambda b_, h: (b_, 0, h)),
        out_shape=jax.ShapeDtypeStruct((b, l, HB * VB), BF16),
        scratch_shapes=[pltpu.VMEM((l, KB), BF16)] * 4
                       + [pltpu.VMEM((l, 2 * KB), BF16),
                          pltpu.VMEM((n, VB, 2 * KB), BF16),
                          pltpu.VMEM((VB, KB), F32), pltpu.VMEM((VB, KB), F32),
                          pltpu.VMEM((2 * A_UNROLL, 2, HG_BLK, KB), BF16),
                          pltpu.VMEM((2 * A_UNROLL, 2, cpb, 1, KB), F32),
                          pltpu.VMEM((2 * B_UNROLL, HG_BLK, HG_BLK), BF16)],
        compiler_params=_cparams(2),
        name="hgrn2",
    )(proj3, proj3, proj3, proj3, proj3, lb2, hnorm, tri_l, tri_u,
      tri_l.astype(F32), tri_u.astype(F32))


def _merge_kernel(x_ref, ua_ref, ub_ref, cq_ref, cz_ref, mkv_ref, ga_ref, gb_ref, gc_ref,
                  wb_ref, wo_ref, pn_ref, o_ref):
    def mem_head(c):
        cols = slice(c * DC, (c + 1) * DC)
        q = cq_ref[:, cols] * jnp.asarray(DC ** -0.5, BF16)
        s = lax.dot_general(q, mkv_ref[0, :, cols], (((1,), (1,)), ((), ())), preferred_element_type=F32)
        e = jnp.exp(s - jnp.max(s, axis=-1, keepdims=True))
        p = e / jnp.sum(e, axis=-1, keepdims=True)
        o = jnp.dot(p.astype(BF16), mkv_ref[0, :, HC * DC + c * DC:HC * DC + (c + 1) * DC],
                    preferred_element_type=F32)
        z = cz_ref[:, cols].astype(F32)
        return (o * (z * jax.nn.sigmoid(z))).astype(BF16)

    def branch(u, g_ref, idx):
        y = jnp.dot(u, wb_ref[idx], preferred_element_type=F32)
        return jax.nn.sigmoid(g_ref[...].astype(F32)) * y

    uc = jnp.concatenate([mem_head(c) for c in range(HC)], axis=1)
    merged = branch(ua_ref[...], ga_ref, 0) + branch(ub_ref[...], gb_ref, 1) + branch(uc, gc_ref, 2)
    y = jnp.dot(merged.astype(BF16), wo_ref[...], preferred_element_type=F32)
    ms = jnp.mean(y * y, axis=-1, keepdims=True)
    o_ref[...] = x_ref[...] + y * lax.rsqrt(ms + EPS) * pn_ref[...]


def _merge(x2d, ua, ub, proj2, mkv3, wb, wo, pn, tm, rows_per_batch):
    m, d = x2d.shape
    col = lambda c: pl.BlockSpec((tm, d), lambda i: (i, c // d))
    tiles_per_batch = rows_per_batch // tm
    return pl.pallas_call(
        _merge_kernel,
        grid=(m // tm,),
        in_specs=[col(0), col(0), col(0), col(COL_CQ), col(COL_CZ),
                  pl.BlockSpec((1, N_MEM, 2 * HC * DC), lambda i: (i // tiles_per_batch, 0, 0)),
                  col(COL_G), col(COL_G + d), col(COL_G + 2 * d),
                  pl.BlockSpec((3, d, d), lambda i: (0, 0, 0)),
                  pl.BlockSpec((d, d), lambda i: (0, 0)),
                  pl.BlockSpec((1, d), lambda i: (0, 0))],
        out_specs=col(0),
        out_shape=jax.ShapeDtypeStruct((m, d), F32),
        compiler_params=_cparams(1),
        name="merge_out",
    )(x2d, ua, ub, proj2, proj2, mkv3, proj2, proj2, proj2, wb, wo, pn)


def _t5_bucket(rel):
    nb = NUM_BUCKETS // 2
    max_exact = nb // 2
    ret = jnp.where(rel > 0, nb, 0)
    n = jnp.abs(rel)
    nf = jnp.maximum(n, 1).astype(jnp.float32)
    large = max_exact + (jnp.log(nf / max_exact) / math.log(MAX_DISTANCE / max_exact)
                         * (nb - max_exact)).astype(jnp.int32)
    large = jnp.minimum(large, nb - 1)
    return ret + jnp.where(n < max_exact, n, large)


def _bias_blocks(rel_bias):
    n, w = BIAS_BLK, 2 * BIAS_BLK
    assert n >= MAX_DISTANCE
    k = jnp.arange(w, dtype=jnp.int32)
    k = jnp.where(k < n, k, k - w)
    o = jnp.arange(-(N_BIAS // 2), N_BIAS // 2 + 1, dtype=jnp.int32)
    u = rel_bias.astype(F32)[_t5_bucket(n * o[:, None] - k[None, :])] * LOG2E
    u = jnp.transpose(u, (2, 0, 1))
    flat = jnp.tile(u, (1, 1, n))[:, :, :n * (w - 1)]
    return flat.reshape(HA, N_BIAS, n, w - 1)[:, :, :, :n]


def kernel(x, mem, pre_norm, post_norm, w_in, lambda_q1, lambda_k1, lambda_q2, lambda_k2,
           diff_subln, rel_bias, lb_logits, hgrn_norm, mem_norm, w_mem_kv, w_branch, w_out):
    b, l, d = x.shape
    t = b * l
    layer = 0
    att_tq, att_tk = min(ATT_TQ, l), min(ATT_TK, l)

    lam_init = 0.8 - 0.6 * math.exp(-0.3 * layer)
    lam = (jnp.exp(jnp.sum(lambda_q1[layer].astype(F32) * lambda_k1[layer].astype(F32)))
           - jnp.exp(jnp.sum(lambda_q2[layer].astype(F32) * lambda_k2[layer].astype(F32)))
           + lam_init)
    assert att_tq % att_tk == 0 and att_tk % BIAS_BLK == 0
    bias_blk = _bias_blocks(rel_bias)
    saturated = bias_blk[:, (0, N_BIAS - 1), 0, 0]
    scalars = jnp.concatenate([lam.reshape(1), saturated.reshape(-1)]).astype(F32)
    lb_all = jnp.cumsum(jax.nn.softmax(lb_logits.astype(F32), axis=1), axis=1)
    lb2 = lb_all[:, layer, :]
    w_in_p = w_in[layer].astype(BF16)
    blk = np.arange(HG_BLK)
    same = (blk[:, None] // CHUNK) == (blk[None, :] // CHUNK)
    tri_l = jnp.asarray(same & (blk[None, :] <= blk[:, None]), BF16)
    tri_u = jnp.asarray(same & (blk[None, :] >= blk[:, None]), BF16)

    proj2 = _norm_matmul(x.reshape(t, d), pre_norm[layer].reshape(1, d), w_in_p, tm=min(1024, t), tn=2048)
    proj3 = proj2.reshape(b, l, IN_COLS)

    ua = _diff_attention(proj3, scalars, bias_blk, diff_subln[layer].reshape(1, 2 * DA).astype(F32),
                         att_tq, att_tk, 1.0 - lam_init)

    ub = _hgrn(proj3, lb2, hgrn_norm[layer].reshape(1, VB).astype(F32), tri_l, tri_u)

    mkv = _norm_matmul(mem.reshape(b * N_MEM, d), mem_norm[layer].reshape(1, d),
                       w_mem_kv[layer].astype(BF16), tm=min(1024, b * N_MEM), tn=2048)
    tm = min(512, l)
    out = _merge(x.reshape(t, d), ua.reshape(t, d), ub.reshape(t, d), proj2,
                 mkv.reshape(b, N_MEM, 2 * HC * DC), w_branch[layer].astype(BF16), w_out[layer].astype(BF16),
                 post_norm[layer].reshape(1, d).astype(F32), tm=tm, rows_per_batch=l)
    return out.reshape(b, l, d)
```

```python
import functools
import math

import jax
import jax.numpy as jnp
import numpy as np
from jax import lax
from jax.experimental import pallas as pl
from jax.experimental.pallas import tpu as pltpu

F32 = jnp.float32
BF16 = jnp.bfloat16

D_MODEL = 1024
N_MEM = 256
HA, DA = 8, 64
HB, KB, VB = 8, 128, 128
HC, DC = 4, 256
NUM_BUCKETS, MAX_DISTANCE = 32, 128
CHUNK = 64
EPS = 1e-6
LANE = 128

COL_AQ, COL_AK, COL_AV, COL_AZ = 0, 1024, 2048, 3072
COL_BQ, COL_BFF, COL_BFB, COL_BI, COL_BZ = 4096, 5120, 6144, 7168, 8192
COL_CQ, COL_CZ, COL_G = 9216, 10240, 11264
IN_COLS = 14336

VMEM_LIMIT = 56 * 1024 * 1024


def _cparams(n_axes, flags=None):
    return pltpu.CompilerParams(dimension_semantics=("arbitrary",) * n_axes,
                                vmem_limit_bytes=VMEM_LIMIT, flags=flags)


def _norm_matmul_kernel(x_ref, nw_ref, w_ref, o_ref, h_ref):
    @pl.when(pl.program_id(1) == 0)
    def _():
        x = x_ref[...]
        ms = jnp.mean(x * x, axis=-1, keepdims=True)
        h_ref[...] = (x * lax.rsqrt(ms + EPS) * nw_ref[...]).astype(BF16)

    o_ref[...] = jnp.dot(h_ref[...], w_ref[...], preferred_element_type=F32).astype(o_ref.dtype)


def _norm_matmul(x2d, norm_w, w_bf16, tm, tn):
    m, d = x2d.shape
    n = w_bf16.shape[1]
    return pl.pallas_call(
        _norm_matmul_kernel,
        grid=(m // tm, n // tn),
        in_specs=[pl.BlockSpec((tm, d), lambda i, j: (i, 0)),
                  pl.BlockSpec((1, d), lambda i, j: (0, 0)),
                  pl.BlockSpec((d, tn), lambda i, j: (0, j))],
        out_specs=pl.BlockSpec((tm, tn), lambda i, j: (i, j)),
        out_shape=jax.ShapeDtypeStruct((m, n), BF16),
        scratch_shapes=[pltpu.VMEM((tm, d), BF16)],
        compiler_params=_cparams(2),
        name="norm_matmul",
    )(x2d, norm_w, w_bf16)


ATT_GROUP = 256
ATT_TQ, ATT_TK = 2048, 512
BIAS_BLK = 128
N_BIAS = 5
LOG2E = math.log2(math.e)
SUM_ROWS = 16


def _attn_kernel(sc_ref, q1_ref, q2_ref, k1_ref, k2_ref, v_ref, z_ref, bias_ref, sub_ref, o_ref,
                 vt_ref, qqt_ref, acc_ref, m_ref, s_ref, mx_ref, c_ref, p_ref, a_ref,
                 *, tq, tk, nk, out_scale):
    h = pl.program_id(1)
    i = pl.program_id(2)
    lam = sc_ref[0]
    ng = 2 * tq // ATT_GROUP
    gpm = tq // ATT_GROUP
    kblks, qblks = tk // BIAS_BLK, ATT_GROUP // BIAS_BLK

    @pl.when(i == 0)
    def _():
        for c in range(nk):
            vt_ref[c, 0:LANE] = v_ref[0, c * tk:(c + 1) * tk, :].astype(F32).T.astype(BF16)
            vt_ref[c, LANE:LANE + SUM_ROWS] = jnp.ones((SUM_ROWS, tk), BF16)

    row = lax.broadcasted_iota(jnp.int32, (LANE, ATT_GROUP), 0)
    keep = (row // DA) == (h % (LANE // DA))
    for mp, q_ref in enumerate((q1_ref, q2_ref)):
        qt = (q_ref[0].astype(F32) * (DA ** -0.5 * LOG2E)).T
        for g in range(gpm):
            piece = qt[:, g * ATT_GROUP:(g + 1) * ATT_GROUP]
            qqt_ref[mp * gpm + g] = jnp.where(keep, piece, 0.0).astype(BF16)

    m_ref[...] = jnp.full(m_ref.shape, -jnp.inf, F32)
    acc_ref[...] = jnp.zeros(acc_ref.shape, F32)

    half = N_BIAS // 2
    c_left, c_right = sc_ref[1 + 2 * h], sc_ref[2 + 2 * h]
    ratio = (tq // BIAS_BLK) // kblks

    def stage_a(j, g, mode):
        qb0 = (g % gpm) * qblks
        mx = None
        for kb in range(kblks):
            rows = pl.ds(pl.multiple_of(j * tk + kb * BIAS_BLK, BIAS_BLK), BIAS_BLK)
            k_ref = k1_ref if g < gpm else k2_ref
            s = jnp.dot(k_ref[0, rows, :], qqt_ref[g], preferred_element_type=F32)
            cols, consts = [], []
            for qb in range(qblks):
                blk = s[:, qb * BIAS_BLK:(qb + 1) * BIAS_BLK]
                if mode == "table":
                    o = j * kblks + kb - (i * (tq // BIAS_BLK) + qb0 + qb)
                    blk, c = blk + bias_ref[0, jnp.clip(o, -half, half) + half], 0.0
                elif mode == "left":
                    c = c_left
                elif mode == "right":
                    c = c_right
                else:
                    o = mode * kblks + kb - (qb0 + qb)
                    if abs(o) < half:
                        blk, c = blk + bias_ref[0, o + half], 0.0
                    else:
                        c = c_left if o < 0 else c_right
                cols.append(blk)
                consts.append(jnp.full((1, BIAS_BLK), c, F32))
            s = jnp.concatenate(cols, axis=1)
            cvec = jnp.concatenate(consts, axis=1)
            s_ref[g % 2, kb * BIAS_BLK:(kb + 1) * BIAS_BLK] = s
            c_ref[g % 2, kb] = cvec
            cur = jnp.max(s, axis=0, keepdims=True) + cvec
            mx = cur if mx is None else jnp.maximum(mx, cur)
        mx_ref[g % 2] = mx

    def stage_b(j, g):
        m_prev = m_ref[g]
        m_new = jnp.maximum(m_prev, mx_ref[g % 2])
        m_ref[g] = m_new
        a_ref[g % 2] = jnp.exp2(m_prev - m_new)
        for kb in range(kblks):
            rows = slice(kb * BIAS_BLK, (kb + 1) * BIAS_BLK)
            p_ref[g % 2, rows] = jnp.exp2(s_ref[g % 2, rows] - (m_new - c_ref[g % 2, kb])).astype(BF16)

    def stage_c(j, g):
        acc_ref[g] = a_ref[g % 2] * acc_ref[g] + jnp.dot(vt_ref[j], p_ref[g % 2],
                                                         preferred_element_type=F32)

    def item_before(j, g, back):
        return (j, g - back) if g >= back else (j - 1, g - back + ng)

    def steps(j, mode, first=False):
        for g in range(ng):
            if not (first and g < 2):
                stage_c(*item_before(j, g, 2))
            if not (first and g < 1):
                stage_b(*item_before(j, g, 1))
            stage_a(j, g, mode)

    def tiles(lo, hi, mode):
        def body(j, carry):
            steps(j, mode)
            return carry
        lax.fori_loop(lo, hi, body, 0)

    steps(0, "table", first=True)
    tiles(1, jnp.maximum(ratio * i - 1, 1), "left")
    for w in range(-1, ratio + 1):
        j_w = ratio * i + w

        @pl.when((j_w >= 1) & (j_w < nk))
        def _():
            steps(j_w, w)
    tiles(jnp.maximum(ratio * i + ratio + 1, 1), nk, "right")
    stage_c(nk - 1, ng - 2)
    stage_b(nk - 1, ng - 1)
    stage_c(nk - 1, ng - 1)

    def normalized(g):
        acc = acc_ref[g]
        return acc[0:LANE] / acc[LANE:LANE + 1]

    d = jnp.concatenate([normalized(g) - lam * normalized(g + gpm) for g in range(gpm)],
                        axis=1).T
    ms = jnp.mean(d * d, axis=-1, keepdims=True)
    y = d * lax.rsqrt(ms + EPS) * sub_ref[...] * out_scale
    z = z_ref[0].astype(F32)
    o_ref[0] = (y * (z * jax.nn.sigmoid(z))).astype(o_ref.dtype)


def _diff_attention(proj3, scalars, bias_blk, subln, tq, tk, out_scale):
    b, l, _ = proj3.shape
    nk = l // tk
    ng = 2 * tq // ATT_GROUP
    cq, ck, cv, cz = COL_AQ // LANE, COL_AK // LANE, COL_AV // LANE, COL_AZ // LANE
    hpb = LANE // DA
    mapw = HA * DA // LANE
    kern = functools.partial(_attn_kernel, tq=tq, tk=tk, nk=nk, out_scale=out_scale)
    return pl.pallas_call(
        kern,
        grid=(b, HA, l // tq),
        in_specs=[pl.BlockSpec(memory_space=pltpu.SMEM),
                  pl.BlockSpec((1, tq, LANE), lambda b_, h, i: (b_, i, cq + h // hpb)),
                  pl.BlockSpec((1, tq, LANE), lambda b_, h, i: (b_, i, cq + mapw + h // hpb)),
                  pl.BlockSpec((1, l, LANE), lambda b_, h, i: (b_, 0, ck + h // hpb)),
                  pl.BlockSpec((1, l, LANE), lambda b_, h, i: (b_, 0, ck + mapw + h // hpb)),
                  pl.BlockSpec((1, l, LANE), lambda b_, h, i: (b_, 0, cv + h)),
                  pl.BlockSpec((1, tq, LANE), lambda b_, h, i: (b_, i, cz + h)),
                  pl.BlockSpec((1, N_BIAS, BIAS_BLK, BIAS_BLK), lambda b_, h, i: (h, 0, 0, 0)),
                  pl.BlockSpec((1, LANE), lambda b_, h, i: (0, 0))],
        out_specs=pl.BlockSpec((1, tq, LANE), lambda b_, h, i: (b_, i, h)),
        out_shape=jax.ShapeDtypeStruct((b, l, HA * 2 * DA), BF16),
        scratch_shapes=[pltpu.VMEM((nk, LANE + SUM_ROWS, tk), BF16),
                        pltpu.VMEM((ng, LANE, ATT_GROUP), BF16),
                        pltpu.VMEM((ng, LANE + SUM_ROWS, ATT_GROUP), F32),
                        pltpu.VMEM((ng, 1, ATT_GROUP), F32),
                        pltpu.VMEM((2, tk, ATT_GROUP), F32),
                        pltpu.VMEM((2, 1, ATT_GROUP), F32),
                        pltpu.VMEM((2, tk // BIAS_BLK, 1, ATT_GROUP), F32),
                        pltpu.VMEM((2, tk, ATT_GROUP), BF16),
                        pltpu.VMEM((2, 1, ATT_GROUP), F32)],
        compiler_params=_cparams(3),
        name="diff_attention",
    )(scalars, proj3, proj3, proj3, proj3, proj3, proj3, bias_blk, subln)


HG_BLK = 256
A_UNROLL, B_UNROLL = 2, 4


def _hgrn_kernel(q_ref, ff_ref, fb_ref, v_ref, z_ref, lb_ref, nw_ref, tl_ref, tu_ref, ml_ref, mu_ref, o_ref,
                 qtf_ref, ktf_ref, qtb_ref, ktb_ref, qh_ref, st_ref, sf_ref, sb_ref,
                 kh_ref, dec_ref, a_ref, *, l):
    nblk = l // HG_BLK
    cpb = HG_BLK // CHUNK
    tn = (((0,), (0,)), ((), ()))
    nt = (((1,), (1,)), ((), ()))
    dirs = ((ff_ref, 0, tl_ref, CHUNK // 2 - 1, CHUNK - 1, qtf_ref, ktf_ref, sf_ref),
            (fb_ref, 1, tu_ref, CHUNK // 2, 0, qtb_ref, ktb_ref, sb_ref))

    def block_of(r, di):
        return r if di == 0 else nblk - 1 - r

    def cumsum(tri, g):
        g_hi = g.astype(BF16)
        g_lo = (g - g_hi.astype(F32)).astype(BF16)
        return (jnp.dot(tri, g_hi, preferred_element_type=F32)
                + jnp.dot(tri, g_lo, preferred_element_type=F32))

    def prep(r):
        for f_ref, di, tri_ref, mid_row, last_row, qt_ref, kt_ref, _ in dirs:
            rows = pl.ds(pl.multiple_of(block_of(r, di) * HG_BLK, HG_BLK), HG_BLK)
            lb = lb_ref[di:di + 1, :]
            x = f_ref[0, rows, :].astype(F32)
            q3 = q_ref[0, rows, :].astype(F32).reshape(cpb, CHUNK, LANE)
            f = lb + (1.0 - lb) * jax.nn.sigmoid(x)
            g = jnp.log(f)
            k3 = (1.0 - f).reshape(cpb, CHUNK, LANE)
            bcs = cumsum(tri_ref[...], g).reshape(cpb, CHUNK, LANE)
            b_mid = bcs[:, mid_row:mid_row + 1, :]
            b_last = bcs[:, last_row:last_row + 1, :]
            qt = q3 * jnp.exp(bcs - b_mid)
            kt = k3 * jnp.exp(b_mid - bcs)
            qh = qt * jnp.exp(b_mid)
            kh = kt * jnp.exp(b_last - b_mid)
            qt_ref[rows, :] = qt.reshape(HG_BLK, LANE).astype(BF16)
            kt_ref[rows, :] = kt.reshape(HG_BLK, LANE).astype(BF16)
            qh_ref[rows, di * KB:(di + 1) * KB] = qh.reshape(HG_BLK, LANE).astype(BF16)
            kh_ref[r % (2 * A_UNROLL), di] = kh.reshape(HG_BLK, LANE).astype(BF16)
            dec_ref[r % (2 * A_UNROLL), di] = jnp.exp(b_last)

    def scan(r):
        for _, di, _, _, _, _, _, s_ref in dirs:
            blk = block_of(r, di)
            s = s_ref[...]
            for c in (range(cpb) if di == 0 else reversed(range(cpb))):
                st_ref[blk * cpb + c, :, di * KB:(di + 1) * KB] = s.astype(BF16)
                vrows = pl.ds(pl.multiple_of(blk * HG_BLK + c * CHUNK, CHUNK), CHUNK)
                u = lax.dot_general(v_ref[0, vrows, :],
                                    kh_ref[r % (2 * A_UNROLL), di, c * CHUNK:(c + 1) * CHUNK, :], tn,
                                    preferred_element_type=F32)
                s = s * dec_ref[r % (2 * A_UNROLL), di, c] + u
            s_ref[...] = s

    def pipelined(first, second, unroll):
        def group(fn, gi):
            for u in range(unroll):
                fn(gi * unroll + u)

        group(first, 0)

        def body(gi, carry):
            group(second, gi - 1)
            group(first, gi)
            return carry

        lax.fori_loop(1, nblk // unroll, body, 0)
        group(second, nblk // unroll - 1)

    sf_ref[...] = jnp.zeros(sf_ref.shape, F32)
    sb_ref[...] = jnp.zeros(sb_ref.shape, F32)
    pipelined(prep, scan, A_UNROLL)

    nw = nw_ref[...]

    def scores(r):
        rows = pl.ds(pl.multiple_of(r * HG_BLK, HG_BLK), HG_BLK)
        a_f = lax.dot_general(qtf_ref[rows, :], ktf_ref[rows, :], nt, preferred_element_type=F32)
        a_b = lax.dot_general(qtb_ref[rows, :], ktb_ref[rows, :], nt, preferred_element_type=F32)
        a = jnp.where(ml_ref[...] != 0.0, a_f, 0.0) + jnp.where(mu_ref[...] != 0.0, a_b, 0.0)
        a_ref[r % (2 * B_UNROLL)] = a.astype(BF16)

    def outputs(r):
        rows = pl.ds(pl.multiple_of(r * HG_BLK, HG_BLK), HG_BLK)
        inter = jnp.concatenate(
            [lax.dot_general(qh_ref[pl.ds(pl.multiple_of(r * HG_BLK + c * CHUNK, CHUNK), CHUNK), :],
                             st_ref[r * cpb + c], nt, preferred_element_type=F32) for c in range(cpb)], axis=0)
        o = jnp.dot(a_ref[r % (2 * B_UNROLL)], v_ref[0, rows, :], preferred_element_type=F32) + inter
        ms = jnp.mean(o * o, axis=-1, keepdims=True)
        y = o * lax.rsqrt(ms + EPS) * nw
        z = z_ref[0, rows, :].astype(F32)
        o_ref[0, rows, :] = (y * (z * jax.nn.sigmoid(z))).astype(o_ref.dtype)

    pipelined(scores, outputs, B_UNROLL)


def _hgrn(proj3, lb2, hnorm, tri_l, tri_u):
    b, l, _ = proj3.shape
    n = l // CHUNK
    cpb = HG_BLK // CHUNK
    c = lambda col: col // LANE
    seq = lambda col: pl.BlockSpec((1, l, LANE), lambda b_, h: (b_, 0, c(col) + h))
    blk = lambda: pl.BlockSpec((HG_BLK, HG_BLK), lambda b_, h: (0, 0))
    return pl.pallas_call(
        functools.partial(_hgrn_kernel, l=l),
        grid=(b, HB),
        in_specs=[seq(COL_BQ), seq(COL_BFF), seq(COL_BFB), seq(COL_BI), seq(COL_BZ),
                  pl.BlockSpec((2, LANE), lambda b_, h: (0, h)),
                  pl.BlockSpec((1, LANE), lambda b_, h: (0, 0)),
                  blk(), blk(), blk(), blk()],
        out_specs=pl.BlockSpec((1, l, LANE), lambda b_, h: (b_, 0, h)),
        out_shape=jax.ShapeDtypeStruct((b, l, HB * VB), BF16),
        scratch_shapes=[pltpu.VMEM((l, KB), BF16)] * 4
                       + [pltpu.VMEM((l, 2 * KB), BF16),
                          pltpu.VMEM((n, VB, 2 * KB), BF16),
                          pltpu.VMEM((VB, KB), F32), pltpu.VMEM((VB, KB), F32),
                          pltpu.VMEM((2 * A_UNROLL, 2, HG_BLK, KB), BF16),
                          pltpu.VMEM((2 * A_UNROLL, 2, cpb, 1, KB), F32),
                          pltpu.VMEM((2 * B_UNROLL, HG_BLK, HG_BLK), BF16)],
        compiler_params=_cparams(2),
        name="hgrn2",
    )(proj3, proj3, proj3, proj3, proj3, lb2, hnorm, tri_l, tri_u,
      tri_l.astype(F32), tri_u.astype(F32))


def _merge_kernel(x_ref, ua_ref, ub_ref, cq_ref, cz_ref, mkv_ref, ga_ref, gb_ref, gc_ref,
                  wb_ref, wo_ref, pn_ref, o_ref):
    def mem_head(c):
        cols = slice(c * DC, (c + 1) * DC)
        q = cq_ref[:, cols] * jnp.asarray(DC ** -0.5, BF16)
        s = lax.dot_general(q, mkv_ref[0, :, cols], (((1,), (1,)), ((), ())), preferred_element_type=F32)
        e = jnp.exp(s - jnp.max(s, axis=-1, keepdims=True))
        p = e / jnp.sum(e, axis=-1, keepdims=True)
        o = jnp.dot(p.astype(BF16), mkv_ref[0, :, HC * DC + c * DC:HC * DC + (c + 1) * DC],
                    preferred_element_type=F32)
        z = cz_ref[:, cols].astype(F32)
        return (o * (z * jax.nn.sigmoid(z))).astype(BF16)

    def branch(u, g_ref, idx):
        y = jnp.dot(u, wb_ref[idx], preferred_element_type=F32)
        return jax.nn.sigmoid(g_ref[...].astype(F32)) * y

    uc = jnp.concatenate([mem_head(c) for c in range(HC)], axis=1)
    merged = branch(ua_ref[...], ga_ref, 0) + branch(ub_ref[...], gb_ref, 1) + branch(uc, gc_ref, 2)
    y = jnp.dot(merged.astype(BF16), wo_ref[...], preferred_element_type=F32)
    ms = jnp.mean(y * y, axis=-1, keepdims=True)
    o_ref[...] = x_ref[...] + y * lax.rsqrt(ms + EPS) * pn_ref[...]


def _merge(x2d, ua, ub, proj2, mkv3, wb, wo, pn, tm, rows_per_batch):
    m, d = x2d.shape
    col = lambda c: pl.BlockSpec((tm, d), lambda i: (i, c // d))
    tiles_per_batch = rows_per_batch // tm
    return pl.pallas_call(
        _merge_kernel,
        grid=(m // tm,),
        in_specs=[col(0), col(0), col(0), col(COL_CQ), col(COL_CZ),
                  pl.BlockSpec((1, N_MEM, 2 * HC * DC), lambda i: (i // tiles_per_batch, 0, 0)),
                  col(COL_G), col(COL_G + d), col(COL_G + 2 * d),
                  pl.BlockSpec((3, d, d), lambda i: (0, 0, 0)),
                  pl.BlockSpec((d, d), lambda i: (0, 0)),
                  pl.BlockSpec((1, d), lambda i: (0, 0))],
        out_specs=col(0),
        out_shape=jax.ShapeDtypeStruct((m, d), F32),
        compiler_params=_cparams(1),
        name="merge_out",
    )(x2d, ua, ub, proj2, proj2, mkv3, proj2, proj2, proj2, wb, wo, pn)


def _t5_bucket(rel):
    nb = NUM_BUCKETS // 2
    max_exact = nb // 2
    ret = jnp.where(rel > 0, nb, 0)
    n = jnp.abs(rel)
    nf = jnp.maximum(n, 1).astype(jnp.float32)
    large = max_exact + (jnp.log(nf / max_exact) / math.log(MAX_DISTANCE / max_exact)
                         * (nb - max_exact)).astype(jnp.int32)
    large = jnp.minimum(large, nb - 1)
    return ret + jnp.where(n < max_exact, n, large)


def _bias_blocks(rel_bias):
    n, w = BIAS_BLK, 2 * BIAS_BLK
    assert n >= MAX_DISTANCE
    k = jnp.arange(w, dtype=jnp.int32)
    k = jnp.where(k < n, k, k - w)
    o = jnp.arange(-(N_BIAS // 2), N_BIAS // 2 + 1, dtype=jnp.int32)
    u = rel_bias.astype(F32)[_t5_bucket(n * o[:, None] - k[None, :])] * LOG2E
    u = jnp.transpose(u, (2, 0, 1))
    flat = jnp.tile(u, (1, 1, n))[:, :, :n * (w - 1)]
    return flat.reshape(HA, N_BIAS, n, w - 1)[:, :, :, :n]


def kernel(x, mem, pre_norm, post_norm, w_in, lambda_q1, lambda_k1, lambda_q2, lambda_k2,
           diff_subln, rel_bias, lb_logits, hgrn_norm, mem_norm, w_mem_kv, w_branch, w_out):
    b, l, d = x.shape
    t = b * l
    layer = 0
    att_tq, att_tk = min(ATT_TQ, l), min(ATT_TK, l)

    lam_init = 0.8 - 0.6 * math.exp(-0.3 * layer)
    lam = (jnp.exp(jnp.sum(lambda_q1[layer].astype(F32) * lambda_k1[layer].astype(F32)))
           - jnp.exp(jnp.sum(lambda_q2[layer].astype(F32) * lambda_k2[layer].astype(F32)))
           + lam_init)
    assert att_tq % att_tk == 0 and att_tk % BIAS_BLK == 0
    bias_blk = _bias_blocks(rel_bias)
    saturated = bias_blk[:, (0, N_BIAS - 1), 0, 0]
    scalars = jnp.concatenate([lam.reshape(1), saturated.reshape(-1)]).astype(F32)
    lb_all = jnp.cumsum(jax.nn.softmax(lb_logits.astype(F32), axis=1), axis=1)
    lb2 = lb_all[:, layer, :]
    w_in_p = w_in[layer].astype(BF16)
    blk = np.arange(HG_BLK)
    same = (blk[:, None] // CHUNK) == (blk[None, :] // CHUNK)
    tri_l = jnp.asarray(same & (blk[None, :] <= blk[:, None]), BF16)
    tri_u = jnp.asarray(same & (blk[None, :] >= blk[:, None]), BF16)

    proj2 = _norm_matmul(x.reshape(t, d), pre_norm[layer].reshape(1, d), w_in_p, tm=min(2048, t), tn=2048)
    proj3 = proj2.reshape(b, l, IN_COLS)

    ua = _diff_attention(proj3, scalars, bias_blk, diff_subln[layer].reshape(1, 2 * DA).astype(F32),
                         att_tq, att_tk, 1.0 - lam_init)

    ub = _hgrn(proj3, lb2, hgrn_norm[layer].reshape(1, VB).astype(F32), tri_l, tri_u)

    mkv = _norm_matmul(mem.reshape(b * N_MEM, d), mem_norm[layer].reshape(1, d),
                       w_mem_kv[layer].astype(BF16), tm=min(1024, b * N_MEM), tn=2048)
    tm = min(512, l)
    out = _merge(x.reshape(t, d), ua.reshape(t, d), ub.reshape(t, d), proj2,
                 mkv.reshape(b, N_MEM, 2 * HC * DC), w_branch[layer].astype(BF16), w_out[layer].astype(BF16),
                 post_norm[layer].reshape(1, d).astype(F32), tm=tm, rows_per_batch=l)
    return out.reshape(b, l, d)
```

```python
import functools
import math

import jax
import jax.numpy as jnp
import numpy as np
from jax import lax
from jax.experimental import pallas as pl
from jax.experimental.pallas import tpu as pltpu

F32 = jnp.float32
BF16 = jnp.bfloat16

D_MODEL = 1024
N_MEM = 256
HA, DA = 8, 64
HB, KB, VB = 8, 128, 128
HC, DC = 4, 256
NUM_BUCKETS, MAX_DISTANCE = 32, 128
CHUNK = 64
EPS = 1e-6
LANE = 128

COL_AQ, COL_AK, COL_AV, COL_AZ = 0, 1024, 2048, 3072
COL_BQ, COL_BFF, COL_BFB, COL_BI, COL_BZ = 4096, 5120, 6144, 7168, 8192
COL_CQ, COL_CZ, COL_G = 9216, 10240, 11264
IN_COLS = 14336

VMEM_LIMIT = 56 * 1024 * 1024


def _cparams(n_axes, flags=None):
    return pltpu.CompilerParams(dimension_semantics=("arbitrary",) * n_axes,
                                vmem_limit_bytes=VMEM_LIMIT, flags=flags)


def _norm_matmul_kernel(x_ref, nw_ref, w_ref, o_ref, h_ref):
    @pl.when(pl.program_id(1) == 0)
    def _():
        x = x_ref[...]
        ms = jnp.mean(x * x, axis=-1, keepdims=True)
        h_ref[...] = (x * lax.rsqrt(ms + EPS) * nw_ref[...]).astype(BF16)

    o_ref[...] = jnp.dot(h_ref[...], w_ref[...], preferred_element_type=F32).astype(o_ref.dtype)


def _norm_matmul(x2d, norm_w, w_bf16, tm, tn):
    m, d = x2d.shape
    n = w_bf16.shape[1]
    return pl.pallas_call(
        _norm_matmul_kernel,
        grid=(m // tm, n // tn),
        in_specs=[pl.BlockSpec((tm, d), lambda i, j: (i, 0)),
                  pl.BlockSpec((1, d), lambda i, j: (0, 0)),
                  pl.BlockSpec((d, tn), lambda i, j: (0, j))],
        out_specs=pl.BlockSpec((tm, tn), lambda i, j: (i, j)),
        out_shape=jax.ShapeDtypeStruct((m, n), BF16),
        scratch_shapes=[pltpu.VMEM((tm, d), BF16)],
        compiler_params=_cparams(2),
        name="norm_matmul",
    )(x2d, norm_w, w_bf16)


ATT_GROUP = 256
ATT_TQ, ATT_TK = 2048, 512
BIAS_BLK = 128
N_BIAS = 5
LOG2E = math.log2(math.e)
SUM_ROWS = 16


def _attn_kernel(sc_ref, q1_ref, q2_ref, k1_ref, k2_ref, v_ref, z_ref, bias_ref, sub_ref, o_ref,
                 vt_ref, qqt_ref, acc_ref, m_ref, s_ref, mx_ref, c_ref, p_ref, a_ref,
                 *, tq, tk, nk, out_scale):
    h = pl.program_id(1)
    i = pl.program_id(2)
    lam = sc_ref[0]
    ng = 2 * tq // ATT_GROUP
    gpm = tq // ATT_GROUP
    kblks, qblks = tk // BIAS_BLK, ATT_GROUP // BIAS_BLK

    @pl.when(i == 0)
    def _():
        for c in range(nk):
            vt_ref[c, 0:LANE] = v_ref[0, c * tk:(c + 1) * tk, :].astype(F32).T.astype(BF16)
            vt_ref[c, LANE:LANE + SUM_ROWS] = jnp.ones((SUM_ROWS, tk), BF16)

    row = lax.broadcasted_iota(jnp.int32, (LANE, ATT_GROUP), 0)
    keep = (row // DA) == (h % (LANE // DA))
    for mp, q_ref in enumerate((q1_ref, q2_ref)):
        qt = (q_ref[0].astype(F32) * (DA ** -0.5 * LOG2E)).T
        for g in range(gpm):
            piece = qt[:, g * ATT_GROUP:(g + 1) * ATT_GROUP]
            qqt_ref[mp * gpm + g] = jnp.where(keep, piece, 0.0).astype(BF16)

    m_ref[...] = jnp.full(m_ref.shape, -jnp.inf, F32)
    acc_ref[...] = jnp.zeros(acc_ref.shape, F32)

    half = N_BIAS // 2
    c_left, c_right = sc_ref[1 + 2 * h], sc_ref[2 + 2 * h]
    ratio = (tq // BIAS_BLK) // kblks

    def stage_a(j, g, mode):
        qb0 = (g % gpm) * qblks
        mx = None
        for kb in range(kblks):
            rows = pl.ds(pl.multiple_of(j * tk + kb * BIAS_BLK, BIAS_BLK), BIAS_BLK)
            k_ref = k1_ref if g < gpm else k2_ref
            s = jnp.dot(k_ref[0, rows, :], qqt_ref[g], preferred_element_type=F32)
            cols, consts = [], []
            for qb in range(qblks):
                blk = s[:, qb * BIAS_BLK:(qb + 1) * BIAS_BLK]
                if mode == "left":
                    c = c_left
                elif mode == "right":
                    c = c_right
                else:
                    o = mode * kblks + kb - (qb0 + qb)
                    if abs(o) < half:
                        blk, c = blk + bias_ref[0, o + half], 0.0
                    else:
                        c = c_left if o < 0 else c_right
                cols.append(blk)
                consts.append(jnp.full((1, BIAS_BLK), c, F32))
            s = jnp.concatenate(cols, axis=1)
            cvec = jnp.concatenate(consts, axis=1)
            s_ref[g % 2, kb * BIAS_BLK:(kb + 1) * BIAS_BLK] = s
            c_ref[g % 2, kb] = cvec
            cur = jnp.max(s, axis=0, keepdims=True) + cvec
            mx = cur if mx is None else jnp.maximum(mx, cur)
        mx_ref[g % 2] = mx

    def stage_b(j, g):
        m_prev = m_ref[g]
        m_new = jnp.maximum(m_prev, mx_ref[g % 2])
        m_ref[g] = m_new
        a_ref[g % 2] = jnp.exp2(m_prev - m_new)
        for kb in range(kblks):
            rows = slice(kb * BIAS_BLK, (kb + 1) * BIAS_BLK)
            p_ref[g % 2, rows] = jnp.exp2(s_ref[g % 2, rows] - (m_new - c_ref[g % 2, kb])).astype(BF16)

    def stage_c(j, g):
        acc_ref[g] = a_ref[g % 2] * acc_ref[g] + jnp.dot(vt_ref[j], p_ref[g % 2],
                                                         preferred_element_type=F32)

    def item_before(j, g, back):
        return (j, g - back) if g >= back else (j - 1, g - back + ng)

    def steps(j, mode, first=False):
        for g in range(ng):
            if not (first and g < 2):
                stage_c(*item_before(j, g, 2))
            if not (first and g < 1):
                stage_b(*item_before(j, g, 1))
            stage_a(j, g, mode)

    def tiles(lo, hi, mode):
        def body(j, carry):
            steps(j, mode)
            return carry
        lax.fori_loop(lo, hi, body, 0)

    @pl.when(i == 0)
    def _():
        steps(0, 0, first=True)

    @pl.when(i > 0)
    def _():
        steps(0, "left", first=True)

    tiles(1, jnp.maximum(ratio * i - 1, 1), "left")
    for w in range(-1, ratio + 1):
        j_w = ratio * i + w

        @pl.when((j_w >= 1) & (j_w < nk))
        def _():
            steps(j_w, w)
    tiles(jnp.maximum(ratio * i + ratio + 1, 1), nk, "right")
    stage_c(nk - 1, ng - 2)
    stage_b(nk - 1, ng - 1)
    stage_c(nk - 1, ng - 1)

    def normalized(g):
        acc = acc_ref[g]
        return acc[0:LANE] / acc[LANE:LANE + 1]

    d = jnp.concatenate([normalized(g) - lam * normalized(g + gpm) for g in range(gpm)],
                        axis=1).T
    ms = jnp.mean(d * d, axis=-1, keepdims=True)
    y = d * lax.rsqrt(ms + EPS) * sub_ref[...] * out_scale
    z = z_ref[0].astype(F32)
    o_ref[0] = (y * (z * jax.nn.sigmoid(z))).astype(o_ref.dtype)


def _diff_attention(proj3, scalars, bias_blk, subln, tq, tk, out_scale):
    b, l, _ = proj3.shape
    nk = l // tk
    ng = 2 * tq // ATT_GROUP
    cq, ck, cv, cz = COL_AQ // LANE, COL_AK // LANE, COL_AV // LANE, COL_AZ // LANE
    hpb = LANE // DA
    mapw = HA * DA // LANE
    kern = functools.partial(_attn_kernel, tq=tq, tk=tk, nk=nk, out_scale=out_scale)
    return pl.pallas_call(
        kern,
        grid=(b, HA, l // tq),
        in_specs=[pl.BlockSpec(memory_space=pltpu.SMEM),
                  pl.BlockSpec((1, tq, LANE), lambda b_, h, i: (b_, i, cq + h // hpb)),
                  pl.BlockSpec((1, tq, LANE), lambda b_, h, i: (b_, i, cq + mapw + h // hpb)),
                  pl.BlockSpec((1, l, LANE), lambda b_, h, i: (b_, 0, ck + h // hpb)),
                  pl.BlockSpec((1, l, LANE), lambda b_, h, i: (b_, 0, ck + mapw + h // hpb)),
                  pl.BlockSpec((1, l, LANE), lambda b_, h, i: (b_, 0, cv + h)),
                  pl.BlockSpec((1, tq, LANE), lambda b_, h, i: (b_, i, cz + h)),
                  pl.BlockSpec((1, N_BIAS, BIAS_BLK, BIAS_BLK), lambda b_, h, i: (h, 0, 0, 0)),
                  pl.BlockSpec((1, LANE), lambda b_, h, i: (0, 0))],
        out_specs=pl.BlockSpec((1, tq, LANE), lambda b_, h, i: (b_, i, h)),
        out_shape=jax.ShapeDtypeStruct((b, l, HA * 2 * DA), BF16),
        scratch_shapes=[pltpu.VMEM((nk, LANE + SUM_ROWS, tk), BF16),
                        pltpu.VMEM((ng, LANE, ATT_GROUP), BF16),
                        pltpu.VMEM((ng, LANE + SUM_ROWS, ATT_GROUP), F32),
                        pltpu.VMEM((ng, 1, ATT_GROUP), F32),
                        pltpu.VMEM((2, tk, ATT_GROUP), F32),
                        pltpu.VMEM((2, 1, ATT_GROUP), F32),
                        pltpu.VMEM((2, tk // BIAS_BLK, 1, ATT_GROUP), F32),
                        pltpu.VMEM((2, tk, ATT_GROUP), BF16),
                        pltpu.VMEM((2, 1, ATT_GROUP), F32)],
        compiler_params=_cparams(3),
        name="diff_attention",
    )(scalars, proj3, proj3, proj3, proj3, proj3, proj3, bias_blk, subln)


HG_BLK = 256
A_UNROLL, B_UNROLL = 2, 8


def _hgrn_kernel(q_ref, ff_ref, fb_ref, v_ref, z_ref, lb_ref, nw_ref, tl_ref, tu_ref, ml_ref, mu_ref, o_ref,
                 qtf_ref, ktf_ref, qtb_ref, ktb_ref, qh_ref, st_ref, sf_ref, sb_ref,
                 kh_ref, dec_ref, a_ref, *, l):
    nblk = l // HG_BLK
    cpb = HG_BLK // CHUNK
    tn = (((0,), (0,)), ((), ()))
    nt = (((1,), (1,)), ((), ()))
    dirs = ((ff_ref, 0, tl_ref, CHUNK // 2 - 1, CHUNK - 1, qtf_ref, ktf_ref, sf_ref),
            (fb_ref, 1, tu_ref, CHUNK // 2, 0, qtb_ref, ktb_ref, sb_ref))

    def block_of(r, di):
        return r if di == 0 else nblk - 1 - r

    def cumsum(tri, g):
        g_hi = g.astype(BF16)
        g_lo = (g - g_hi.astype(F32)).astype(BF16)
        return (jnp.dot(tri, g_hi, preferred_element_type=F32)
                + jnp.dot(tri, g_lo, preferred_element_type=F32))

    def prep(r):
        for f_ref, di, tri_ref, mid_row, last_row, qt_ref, kt_ref, _ in dirs:
            rows = pl.ds(pl.multiple_of(block_of(r, di) * HG_BLK, HG_BLK), HG_BLK)
            lb = lb_ref[di:di + 1, :]
            x = f_ref[0, rows, :].astype(F32)
            q3 = q_ref[0, rows, :].astype(F32).reshape(cpb, CHUNK, LANE)
            f = lb + (1.0 - lb) * jax.nn.sigmoid(x)
            g = jnp.log(f)
            k3 = (1.0 - f).reshape(cpb, CHUNK, LANE)
            bcs = cumsum(tri_ref[...], g).reshape(cpb, CHUNK, LANE)
            b_mid = bcs[:, mid_row:mid_row + 1, :]
            b_last = bcs[:, last_row:last_row + 1, :]
            qt = q3 * jnp.exp(bcs - b_mid)
            kt = k3 * jnp.exp(b_mid - bcs)
            qh = qt * jnp.exp(b_mid)
            kh = kt * jnp.exp(b_last - b_mid)
            qt_ref[rows, :] = qt.reshape(HG_BLK, LANE).astype(BF16)
            kt_ref[rows, :] = kt.reshape(HG_BLK, LANE).astype(BF16)
            qh_ref[rows, di * KB:(di + 1) * KB] = qh.reshape(HG_BLK, LANE).astype(BF16)
            kh_ref[r % (2 * A_UNROLL), di] = kh.reshape(HG_BLK, LANE).astype(BF16)
            dec_ref[r % (2 * A_UNROLL), di] = jnp.exp(b_last)

    def scan(r):
        for _, di, _, _, _, _, _, s_ref in dirs:
            blk = block_of(r, di)
            s = s_ref[...]
            for c in (range(cpb) if di == 0 else reversed(range(cpb))):
                st_ref[blk * cpb + c, :, di * KB:(di + 1) * KB] = s.astype(BF16)
                vrows = pl.ds(pl.multiple_of(blk * HG_BLK + c * CHUNK, CHUNK), CHUNK)
                u = lax.dot_general(v_ref[0, vrows, :],
                                    kh_ref[r % (2 * A_UNROLL), di, c * CHUNK:(c + 1) * CHUNK, :], tn,
                                    preferred_element_type=F32)
                s = s * dec_ref[r % (2 * A_UNROLL), di, c] + u
            s_ref[...] = s

    def pipelined(first, second, unroll):
        def group(fn, gi):
            for u in range(unroll):
                fn(gi * unroll + u)

        group(first, 0)

        def body(gi, carry):
            group(second, gi - 1)
            group(first, gi)
            return carry

        lax.fori_loop(1, nblk // unroll, body, 0)
        group(second, nblk // unroll - 1)

    sf_ref[...] = jnp.zeros(sf_ref.shape, F32)
    sb_ref[...] = jnp.zeros(sb_ref.shape, F32)
    pipelined(prep, scan, A_UNROLL)

    nw = nw_ref[...]

    def scores(r):
        rows = pl.ds(pl.multiple_of(r * HG_BLK, HG_BLK), HG_BLK)
        a_f = lax.dot_general(qtf_ref[rows, :], ktf_ref[rows, :], nt, preferred_element_type=F32)
        a_b = lax.dot_general(qtb_ref[rows, :], ktb_ref[rows, :], nt, preferred_element_type=F32)
        a = jnp.where(ml_ref[...] != 0.0, a_f, 0.0) + jnp.where(mu_ref[...] != 0.0, a_b, 0.0)
        a_ref[r % (2 * B_UNROLL)] = a.astype(BF16)

    def outputs(r):
        rows = pl.ds(pl.multiple_of(r * HG_BLK, HG_BLK), HG_BLK)
        inter = jnp.concatenate(
            [lax.dot_general(qh_ref[pl.ds(pl.multiple_of(r * HG_BLK + c * CHUNK, CHUNK), CHUNK), :],
                             st_ref[r * cpb + c], nt, preferred_element_type=F32) for c in range(cpb)], axis=0)
        o = jnp.dot(a_ref[r % (2 * B_UNROLL)], v_ref[0, rows, :], preferred_element_type=F32) + inter
        ms = jnp.mean(o * o, axis=-1, keepdims=True)
        y = o * lax.rsqrt(ms + EPS) * nw
        z = z_ref[0, rows, :].astype(F32)
        o_ref[0, rows, :] = (y * (z * jax.nn.sigmoid(z))).astype(o_ref.dtype)

    pipelined(scores, outputs, B_UNROLL)


def _hgrn(proj3, lb2, hnorm, tri_l, tri_u):
    b, l, _ = proj3.shape
    assert l % (HG_BLK * A_UNROLL) == 0 and l % (HG_BLK * B_UNROLL) == 0
    n = l // CHUNK
    cpb = HG_BLK // CHUNK
    c = lambda col: col // LANE
    seq = lambda col: pl.BlockSpec((1, l, LANE), lambda b_, h: (b_, 0, c(col) + h))
    blk = lambda: pl.BlockSpec((HG_BLK, HG_BLK), lambda b_, h: (0, 0))
    return pl.pallas_call(
        functools.partial(_hgrn_kernel, l=l),
        grid=(b, HB),
        in_specs=[seq(COL_BQ), seq(COL_BFF), seq(COL_BFB), seq(COL_BI), seq(COL_BZ),
                  pl.BlockSpec((2, LANE), lambda b_, h: (0, h)),
                  pl.BlockSpec((1, LANE), lambda b_, h: (0, 0)),
                  blk(), blk(), blk(), blk()],
        out_specs=pl.BlockSpec((1, l, LANE), lambda b_, h: (b_, 0, h)),
        out_shape=jax.ShapeDtypeStruct((b, l, HB * VB), BF16),
        scratch_shapes=[pltpu.VMEM((l, KB), BF16)] * 4
                       + [pltpu.VMEM((l, 2 * KB), BF16),
                          pltpu.VMEM((n, VB, 2 * KB), BF16),
                          pltpu.VMEM((VB, KB), F32), pltpu.VMEM((VB, KB), F32),
                          pltpu.VMEM((2 * A_UNROLL, 2, HG_BLK, KB), BF16),
                          pltpu.VMEM((2 * A_UNROLL, 2, cpb, 1, KB), F32),
                          pltpu.VMEM((2 * B_UNROLL, HG_BLK, HG_BLK), BF16)],
        compiler_params=_cparams(2),
        name="hgrn2",
    )(proj3, proj3, proj3, proj3, proj3, lb2, hnorm, tri_l, tri_u,
      tri_l.astype(F32), tri_u.astype(F32))


def _merge_kernel(x_ref, ua_ref, ub_ref, cq_ref, cz_ref, mkv_ref, ga_ref, gb_ref, gc_ref,
                  wb_ref, wo_ref, pn_ref, o_ref):
    def mem_head(c):
        cols = slice(c * DC, (c + 1) * DC)
        q = cq_ref[:, cols] * jnp.asarray(DC ** -0.5, BF16)
        s = lax.dot_general(q, mkv_ref[0, :, cols], (((1,), (1,)), ((), ())), preferred_element_type=F32)
        e = jnp.exp(s - jnp.max(s, axis=-1, keepdims=True))
        p = e / jnp.sum(e, axis=-1, keepdims=True)
        o = jnp.dot(p.astype(BF16), mkv_ref[0, :, HC * DC + c * DC:HC * DC + (c + 1) * DC],
                    preferred_element_type=F32)
        z = cz_ref[:, cols].astype(F32)
        return (o * (z * jax.nn.sigmoid(z))).astype(BF16)

    def branch(u, g_ref, idx):
        y = jnp.dot(u, wb_ref[idx], preferred_element_type=F32)
        return jax.nn.sigmoid(g_ref[...].astype(F32)) * y

    uc = jnp.concatenate([mem_head(c) for c in range(HC)], axis=1)
    merged = branch(ua_ref[...], ga_ref, 0) + branch(ub_ref[...], gb_ref, 1) + branch(uc, gc_ref, 2)
    y = jnp.dot(merged.astype(BF16), wo_ref[...], preferred_element_type=F32)
    ms = jnp.mean(y * y, axis=-1, keepdims=True)
    o_ref[...] = x_ref[...] + y * lax.rsqrt(ms + EPS) * pn_ref[...]


def _merge(x2d, ua, ub, proj2, mkv3, wb, wo, pn, tm, rows_per_batch):
    m, d = x2d.shape
    col = lambda c: pl.BlockSpec((tm, d), lambda i: (i, c // d))
    tiles_per_batch = rows_per_batch // tm
    return pl.pallas_call(
        _merge_kernel,
        grid=(m // tm,),
        in_specs=[col(0), col(0), col(0), col(COL_CQ), col(COL_CZ),
                  pl.BlockSpec((1, N_MEM, 2 * HC * DC), lambda i: (i // tiles_per_batch, 0, 0)),
                  col(COL_G), col(COL_G + d), col(COL_G + 2 * d),
                  pl.BlockSpec((3, d, d), lambda i: (0, 0, 0)),
                  pl.BlockSpec((d, d), lambda i: (0, 0)),
                  pl.BlockSpec((1, d), lambda i: (0, 0))],
        out_specs=col(0),
        out_shape=jax.ShapeDtypeStruct((m, d), F32),
        compiler_params=_cparams(1),
        name="merge_out",
    )(x2d, ua, ub, proj2, proj2, mkv3, proj2, proj2, proj2, wb, wo, pn)


def _t5_bucket(rel):
    nb = NUM_BUCKETS // 2
    max_exact = nb // 2
    ret = jnp.where(rel > 0, nb, 0)
    n = jnp.abs(rel)
    nf = jnp.maximum(n, 1).astype(jnp.float32)
    large = max_exact + (jnp.log(nf / max_exact) / math.log(MAX_DISTANCE / max_exact)
                         * (nb - max_exact)).astype(jnp.int32)
    large = jnp.minimum(large, nb - 1)
    return ret + jnp.where(n < max_exact, n, large)


def _bias_blocks(rel_bias):
    n, w = BIAS_BLK, 2 * BIAS_BLK
    assert n >= MAX_DISTANCE
    k = jnp.arange(w, dtype=jnp.int32)
    k = jnp.where(k < n, k, k - w)
    o = jnp.arange(-(N_BIAS // 2), N_BIAS // 2 + 1, dtype=jnp.int32)
    u = rel_bias.astype(F32)[_t5_bucket(n * o[:, None] - k[None, :])] * LOG2E
    u = jnp.transpose(u, (2, 0, 1))
    flat = jnp.tile(u, (1, 1, n))[:, :, :n * (w - 1)]
    return flat.reshape(HA, N_BIAS, n, w - 1)[:, :, :, :n]


def kernel(x, mem, pre_norm, post_norm, w_in, lambda_q1, lambda_k1, lambda_q2, lambda_k2,
           diff_subln, rel_bias, lb_logits, hgrn_norm, mem_norm, w_mem_kv, w_branch, w_out):
    b, l, d = x.shape
    t = b * l
    layer = 0
    att_tq, att_tk = min(ATT_TQ, l), min(ATT_TK, l)

    lam_init = 0.8 - 0.6 * math.exp(-0.3 * layer)
    lam = (jnp.exp(jnp.sum(lambda_q1[layer].astype(F32) * lambda_k1[layer].astype(F32)))
           - jnp.exp(jnp.sum(lambda_q2[layer].astype(F32) * lambda_k2[layer].astype(F32)))
           + lam_init)
    assert att_tq % att_tk == 0 and att_tq >= 2 * att_tk and att_tk % BIAS_BLK == 0
    bias_blk = _bias_blocks(rel_bias)
    saturated = bias_blk[:, (0, N_BIAS - 1), 0, 0]
    scalars = jnp.concatenate([lam.reshape(1), saturated.reshape(-1)]).astype(F32)
    lb_all = jnp.cumsum(jax.nn.softmax(lb_logits.astype(F32), axis=1), axis=1)
    lb2 = lb_all[:, layer, :]
    w_in_p = w_in[layer].astype(BF16)
    blk = np.arange(HG_BLK)
    same = (blk[:, None] // CHUNK) == (blk[None, :] // CHUNK)
    tri_l = jnp.asarray(same & (blk[None, :] <= blk[:, None]), BF16)
    tri_u = jnp.asarray(same & (blk[None, :] >= blk[:, None]), BF16)

    proj2 = _norm_matmul(x.reshape(t, d), pre_norm[layer].reshape(1, d), w_in_p, tm=min(2048, t), tn=2048)
    proj3 = proj2.reshape(b, l, IN_COLS)

    ua = _diff_attention(proj3, scalars, bias_blk, diff_subln[layer].reshape(1, 2 * DA).astype(F32),
                         att_tq, att_tk, 1.0 - lam_init)

    ub = _hgrn(proj3, lb2, hgrn_norm[layer].reshape(1, VB).astype(F32), tri_l, tri_u)

    mkv = _norm_matmul(mem.reshape(b * N_MEM, d), mem_norm[layer].reshape(1, d),
                       w_mem_kv[layer].astype(BF16), tm=min(1024, b * N_MEM), tn=2048)
    tm = min(512, l)
    out = _merge(x.reshape(t, d), ua.reshape(t, d), ub.reshape(t, d), proj2,
                 mkv.reshape(b, N_MEM, 2 * HC * DC), w_branch[layer].astype(BF16), w_out[layer].astype(BF16),
                 post_norm[layer].reshape(1, d).astype(F32), tm=tm, rows_per_batch=l)
    return out.reshape(b, l, d)
```

```python
import functools
import math

import jax
import jax.numpy as jnp
import numpy as np
from jax import lax
from jax.experimental import pallas as pl
from jax.experimental.pallas import tpu as pltpu

F32 = jnp.float32
BF16 = jnp.bfloat16

D_MODEL = 1024
N_MEM = 256
HA, DA = 8, 64
HB, KB, VB = 8, 128, 128
HC, DC = 4, 256
NUM_BUCKETS, MAX_DISTANCE = 32, 128
CHUNK = 64
EPS = 1e-6
LANE = 128

COL_AQ, COL_AK, COL_AV, COL_AZ = 0, 1024, 2048, 3072
COL_BQ, COL_BFF, COL_BFB, COL_BI, COL_BZ = 4096, 5120, 6144, 7168, 8192
COL_CQ, COL_CZ, COL_G = 9216, 10240, 11264
IN_COLS = 14336

VMEM_LIMIT = 56 * 1024 * 1024


def _cparams(n_axes, flags=None):
    return pltpu.CompilerParams(dimension_semantics=("arbitrary",) * n_axes,
                                vmem_limit_bytes=VMEM_LIMIT, flags=flags)


def _norm_matmul_kernel(x_ref, nw_ref, w_ref, o_ref, h_ref):
    @pl.when(pl.program_id(1) == 0)
    def _():
        x = x_ref[...]
        ms = jnp.mean(x * x, axis=-1, keepdims=True)
        h_ref[...] = (x * lax.rsqrt(ms + EPS) * nw_ref[...]).astype(BF16)

    o_ref[...] = jnp.dot(h_ref[...], w_ref[...], preferred_element_type=F32).astype(o_ref.dtype)


def _norm_matmul(x2d, norm_w, w_bf16, tm, tn):
    m, d = x2d.shape
    n = w_bf16.shape[1]
    return pl.pallas_call(
        _norm_matmul_kernel,
        grid=(m // tm, n // tn),
        in_specs=[pl.BlockSpec((tm, d), lambda i, j: (i, 0)),
                  pl.BlockSpec((1, d), lambda i, j: (0, 0)),
                  pl.BlockSpec((d, tn), lambda i, j: (0, j))],
        out_specs=pl.BlockSpec((tm, tn), lambda i, j: (i, j)),
        out_shape=jax.ShapeDtypeStruct((m, n), BF16),
        scratch_shapes=[pltpu.VMEM((tm, d), BF16)],
        compiler_params=_cparams(2),
        name="norm_matmul",
    )(x2d, norm_w, w_bf16)


ATT_GROUP = 256
ATT_TQ, ATT_TK = 2048, 512
BIAS_BLK = 128
N_BIAS = 5
LOG2E = math.log2(math.e)
SUM_ROWS = 16


def _attn_kernel(sc_ref, q1_ref, q2_ref, k1_ref, k2_ref, v_ref, z_ref, bias_ref, sub_ref, o_ref,
                 vt_ref, qqt_ref, acc_ref, m_ref, s_ref, mx_ref, c_ref, p_ref, a_ref,
                 *, tq, tk, nk, out_scale):
    h = pl.program_id(1)
    i = pl.program_id(2)
    lam = sc_ref[0]
    ng = 2 * tq // ATT_GROUP
    gpm = tq // ATT_GROUP
    kblks, qblks = tk // BIAS_BLK, ATT_GROUP // BIAS_BLK

    @pl.when(i == 0)
    def _():
        for c in range(nk):
            vt_ref[c, 0:LANE] = v_ref[0, c * tk:(c + 1) * tk, :].astype(F32).T.astype(BF16)
            vt_ref[c, LANE:LANE + SUM_ROWS] = jnp.ones((SUM_ROWS, tk), BF16)

    row = lax.broadcasted_iota(jnp.int32, (LANE, ATT_GROUP), 0)
    keep = (row // DA) == (h % (LANE // DA))
    for mp, q_ref in enumerate((q1_ref, q2_ref)):
        qt = (q_ref[0].astype(F32) * (DA ** -0.5 * LOG2E)).T
        for g in range(gpm):
            piece = qt[:, g * ATT_GROUP:(g + 1) * ATT_GROUP]
            qqt_ref[mp * gpm + g] = jnp.where(keep, piece, 0.0).astype(BF16)

    m_ref[...] = jnp.full(m_ref.shape, -jnp.inf, F32)
    acc_ref[...] = jnp.zeros(acc_ref.shape, F32)

    half = N_BIAS // 2
    c_left, c_right = sc_ref[1 + 2 * h], sc_ref[2 + 2 * h]
    ratio = (tq // BIAS_BLK) // kblks

    def stage_a(j, g, mode):
        qb0 = (g % gpm) * qblks
        mx = None
        for kb in range(kblks):
            rows = pl.ds(pl.multiple_of(j * tk + kb * BIAS_BLK, BIAS_BLK), BIAS_BLK)
            k_ref = k1_ref if g < gpm else k2_ref
            s = jnp.dot(k_ref[0, rows, :], qqt_ref[g], preferred_element_type=F32)
            cols, consts = [], []
            for qb in range(qblks):
                blk = s[:, qb * BIAS_BLK:(qb + 1) * BIAS_BLK]
                if mode == "table":
                    o = j * kblks + kb - (i * (tq // BIAS_BLK) + qb0 + qb)
                    blk, c = blk + bias_ref[0, jnp.clip(o, -half, half) + half], 0.0
                elif mode == "left":
                    c = c_left
                elif mode == "right":
                    c = c_right
                else:
                    o = mode * kblks + kb - (qb0 + qb)
                    if abs(o) < half:
                        blk, c = blk + bias_ref[0, o + half], 0.0
                    else:
                        c = c_left if o < 0 else c_right
                cols.append(blk)
                consts.append(jnp.full((1, BIAS_BLK), c, F32))
            s = jnp.concatenate(cols, axis=1)
            cvec = jnp.concatenate(consts, axis=1)
            s_ref[g % 2, kb * BIAS_BLK:(kb + 1) * BIAS_BLK] = s
            c_ref[g % 2, kb] = cvec
            cur = jnp.max(s, axis=0, keepdims=True) + cvec
            mx = cur if mx is None else jnp.maximum(mx, cur)
        mx_ref[g % 2] = mx

    def stage_b(j, g):
        m_prev = m_ref[g]
        m_new = jnp.maximum(m_prev, mx_ref[g % 2])
        m_ref[g] = m_new
        a_ref[g % 2] = jnp.exp2(m_prev - m_new)
        for kb in range(kblks):
            rows = slice(kb * BIAS_BLK, (kb + 1) * BIAS_BLK)
            p_ref[g % 2, rows] = jnp.exp2(s_ref[g % 2, rows] - (m_new - c_ref[g % 2, kb])).astype(BF16)

    def stage_c(j, g):
        acc_ref[g] = a_ref[g % 2] * acc_ref[g] + jnp.dot(vt_ref[j], p_ref[g % 2],
                                                         preferred_element_type=F32)

    def item_before(j, g, back):
        return (j, g - back) if g >= back else (j - 1, g - back + ng)

    def steps(j, mode, first=False):
        for g in range(ng):
            if not (first and g < 2):
                stage_c(*item_before(j, g, 2))
            if not (first and g < 1):
                stage_b(*item_before(j, g, 1))
            stage_a(j, g, mode)

    def tiles(lo, hi, mode):
        def body(j, carry):
            steps(j, mode)
            return carry
        lax.fori_loop(lo, hi, body, 0)

    steps(0, "table", first=True)
    tiles(1, jnp.maximum(ratio * i - 1, 1), "left")
    for w in range(-1, ratio + 1):
        j_w = ratio * i + w

        @pl.when((j_w >= 1) & (j_w < nk))
        def _():
            steps(j_w, w)
    tiles(jnp.maximum(ratio * i + ratio + 1, 1), nk, "right")
    stage_c(nk - 1, ng - 2)
    stage_b(nk - 1, ng - 1)
    stage_c(nk - 1, ng - 1)

    def normalized(g):
        acc = acc_ref[g]
        return acc[0:LANE] / acc[LANE:LANE + 1]

    d = jnp.concatenate([normalized(g) - lam * normalized(g + gpm) for g in range(gpm)],
                        axis=1).T
    ms = jnp.mean(d * d, axis=-1, keepdims=True)
    y = d * lax.rsqrt(ms + EPS) * sub_ref[...] * out_scale
    z = z_ref[0].astype(F32)
    o_ref[0] = (y * (z * jax.nn.sigmoid(z))).astype(o_ref.dtype)


def _diff_attention(proj3, scalars, bias_blk, subln, tq, tk, out_scale):
    b, l, _ = proj3.shape
    nk = l // tk
    ng = 2 * tq // ATT_GROUP
    cq, ck, cv, cz = COL_AQ // LANE, COL_AK // LANE, COL_AV // LANE, COL_AZ // LANE
    hpb = LANE // DA
    mapw = HA * DA // LANE
    kern = functools.partial(_attn_kernel, tq=tq, tk=tk, nk=nk, out_scale=out_scale)
    return pl.pallas_call(
        kern,
        grid=(b, HA, l // tq),
        in_specs=[pl.BlockSpec(memory_space=pltpu.SMEM),
                  pl.BlockSpec((1, tq, LANE), lambda b_, h, i: (b_, i, cq + h // hpb)),
                  pl.BlockSpec((1, tq, LANE), lambda b_, h, i: (b_, i, cq + mapw + h // hpb)),
                  pl.BlockSpec((1, l, LANE), lambda b_, h, i: (b_, 0, ck + h // hpb)),
                  pl.BlockSpec((1, l, LANE), lambda b_, h, i: (b_, 0, ck + mapw + h // hpb)),
                  pl.BlockSpec((1, l, LANE), lambda b_, h, i: (b_, 0, cv + h)),
                  pl.BlockSpec((1, tq, LANE), lambda b_, h, i: (b_, i, cz + h)),
                  pl.BlockSpec((1, N_BIAS, BIAS_BLK, BIAS_BLK), lambda b_, h, i: (h, 0, 0, 0)),
                  pl.BlockSpec((1, LANE), lambda b_, h, i: (0, 0))],
        out_specs=pl.BlockSpec((1, tq, LANE), lambda b_, h, i: (b_, i, h)),
        out_shape=jax.ShapeDtypeStruct((b, l, HA * 2 * DA), BF16),
        scratch_shapes=[pltpu.VMEM((nk, LANE + SUM_ROWS, tk), BF16),
                        pltpu.VMEM((ng, LANE, ATT_GROUP), BF16),
                        pltpu.VMEM((ng, LANE + SUM_ROWS, ATT_GROUP), F32),
                        pltpu.VMEM((ng, 1, ATT_GROUP), F32),
                        pltpu.VMEM((2, tk, ATT_GROUP), F32),
                        pltpu.VMEM((2, 1, ATT_GROUP), F32),
                        pltpu.VMEM((2, tk // BIAS_BLK, 1, ATT_GROUP), F32),
                        pltpu.VMEM((2, tk, ATT_GROUP), BF16),
                        pltpu.VMEM((2, 1, ATT_GROUP), F32)],
        compiler_params=_cparams(3),
        name="diff_attention",
    )(scalars, proj3, proj3, proj3, proj3, proj3, proj3, bias_blk, subln)


HG_BLK = 256
A_UNROLL, B_UNROLL = 2, 8


def _hgrn_kernel(q_ref, ff_ref, fb_ref, v_ref, z_ref, lb_ref, nw_ref, tl_ref, tu_ref, ml_ref, mu_ref, o_ref,
                 qtf_ref, ktf_ref, qtb_ref, ktb_ref, qh_ref, st_ref, sf_ref, sb_ref,
                 kh_ref, dec_ref, a_ref, *, l):
    nblk = l // HG_BLK
    cpb = HG_BLK // CHUNK
    tn = (((0,), (0,)), ((), ()))
    nt = (((1,), (1,)), ((), ()))
    dirs = ((ff_ref, 0, tl_ref, CHUNK // 2 - 1, CHUNK - 1, qtf_ref, ktf_ref, sf_ref),
            (fb_ref, 1, tu_ref, CHUNK // 2, 0, qtb_ref, ktb_ref, sb_ref))

    def block_of(r, di):
        return r if di == 0 else nblk - 1 - r

    def cumsum(tri, g):
        g_hi = g.astype(BF16)
        g_lo = (g - g_hi.astype(F32)).astype(BF16)
        return (jnp.dot(tri, g_hi, preferred_element_type=F32)
                + jnp.dot(tri, g_lo, preferred_element_type=F32))

    def prep(r):
        for f_ref, di, tri_ref, mid_row, last_row, qt_ref, kt_ref, _ in dirs:
            rows = pl.ds(pl.multiple_of(block_of(r, di) * HG_BLK, HG_BLK), HG_BLK)
            lb = lb_ref[di:di + 1, :]
            x = f_ref[0, rows, :].astype(F32)
            q3 = q_ref[0, rows, :].astype(F32).reshape(cpb, CHUNK, LANE)
            f = lb + (1.0 - lb) * jax.nn.sigmoid(x)
            g = jnp.log(f)
            k3 = (1.0 - f).reshape(cpb, CHUNK, LANE)
            bcs = cumsum(tri_ref[...], g).reshape(cpb, CHUNK, LANE)
            b_mid = bcs[:, mid_row:mid_row + 1, :]
            b_last = bcs[:, last_row:last_row + 1, :]
            qt = q3 * jnp.exp(bcs - b_mid)
            kt = k3 * jnp.exp(b_mid - bcs)
            qh = qt * jnp.exp(b_mid)
            kh = kt * jnp.exp(b_last - b_mid)
            qt_ref[rows, :] = qt.reshape(HG_BLK, LANE).astype(BF16)
            kt_ref[rows, :] = kt.reshape(HG_BLK, LANE).astype(BF16)
            qh_ref[rows, di * KB:(di + 1) * KB] = qh.reshape(HG_BLK, LANE).astype(BF16)
            kh_ref[r % (2 * A_UNROLL), di] = kh.reshape(HG_BLK, LANE).astype(BF16)
            dec_ref[r % (2 * A_UNROLL), di] = jnp.exp(b_last)

    def scan(r):
        for _, di, _, _, _, _, _, s_ref in dirs:
            blk = block_of(r, di)
            s = s_ref[...]
            for c in (range(cpb) if di == 0 else reversed(range(cpb))):
                st_ref[blk * cpb + c, :, di * KB:(di + 1) * KB] = s.astype(BF16)
                vrows = pl.ds(pl.multiple_of(blk * HG_BLK + c * CHUNK, CHUNK), CHUNK)
                u = lax.dot_general(v_ref[0, vrows, :],
                                    kh_ref[r % (2 * A_UNROLL), di, c * CHUNK:(c + 1) * CHUNK, :], tn,
                                    preferred_element_type=F32)
                s = s * dec_ref[r % (2 * A_UNROLL), di, c] + u
            s_ref[...] = s

    def pipelined(first, second, unroll):
        def group(fn, gi):
            for u in range(unroll):
                fn(gi * unroll + u)

        group(first, 0)

        def body(gi, carry):
            group(second, gi - 1)
            group(first, gi)
            return carry

        lax.fori_loop(1, nblk // unroll, body, 0)
        group(second, nblk // unroll - 1)

    sf_ref[...] = jnp.zeros(sf_ref.shape, F32)
    sb_ref[...] = jnp.zeros(sb_ref.shape, F32)
    pipelined(prep, scan, A_UNROLL)

    nw = nw_ref[...]

    def scores(r):
        rows = pl.ds(pl.multiple_of(r * HG_BLK, HG_BLK), HG_BLK)
        a_f = lax.dot_general(qtf_ref[rows, :], ktf_ref[rows, :], nt, preferred_element_type=F32)
        a_b = lax.dot_general(qtb_ref[rows, :], ktb_ref[rows, :], nt, preferred_element_type=F32)
        a = jnp.where(ml_ref[...] != 0.0, a_f, 0.0) + jnp.where(mu_ref[...] != 0.0, a_b, 0.0)
        a_ref[r % (2 * B_UNROLL)] = a.astype(BF16)

    def outputs(r):
        rows = pl.ds(pl.multiple_of(r * HG_BLK, HG_BLK), HG_BLK)
        inter = jnp.concatenate(
            [lax.dot_general(qh_ref[pl.ds(pl.multiple_of(r * HG_BLK + c * CHUNK, CHUNK), CHUNK), :],
                             st_ref[r * cpb + c], nt, preferred_element_type=F32) for c in range(cpb)], axis=0)
        o = jnp.dot(a_ref[r % (2 * B_UNROLL)], v_ref[0, rows, :], preferred_element_type=F32) + inter
        ms = jnp.mean(o * o, axis=-1, keepdims=True)
        y = o * lax.rsqrt(ms + EPS) * nw
        z = z_ref[0, rows, :].astype(F32)
        o_ref[0, rows, :] = (y * (z * jax.nn.sigmoid(z))).astype(o_ref.dtype)

    pipelined(scores, outputs, B_UNROLL)


def _hgrn(proj3, lb2, hnorm, tri_l, tri_u):
    b, l, _ = proj3.shape
    assert l % (HG_BLK * A_UNROLL) == 0 and l % (HG_BLK * B_UNROLL) == 0
    n = l // CHUNK
    cpb = HG_BLK // CHUNK
    c = lambda col: col // LANE
    seq = lambda col: pl.BlockSpec((1, l, LANE), lambda b_, h: (b_, 0, c(col) + h))
    blk = lambda: pl.BlockSpec((HG_BLK, HG_BLK), lambda b_, h: (0, 0))
    return pl.pallas_call(
        functools.partial(_hgrn_kernel, l=l),
        grid=(b, HB),
        in_specs=[seq(COL_BQ), seq(COL_BFF), seq(COL_BFB), seq(COL_BI), seq(COL_BZ),
                  pl.BlockSpec((2, LANE), lambda b_, h: (0, h)),
                  pl.BlockSpec((1, LANE), lambda b_, h: (0, 0)),
                  blk(), blk(), blk(), blk()],
        out_specs=pl.BlockSpec((1, l, LANE), lambda b_, h: (b_, 0, h)),
        out_shape=jax.ShapeDtypeStruct((b, l, HB * VB), BF16),
        scratch_shapes=[pltpu.VMEM((l, KB), BF16)] * 4
                       + [pltpu.VMEM((l, 2 * KB), BF16),
                          pltpu.VMEM((n, VB, 2 * KB), BF16),
                          pltpu.VMEM((VB, KB), F32), pltpu.VMEM((VB, KB), F32),
                          pltpu.VMEM((2 * A_UNROLL, 2, HG_BLK, KB), BF16),
                          pltpu.VMEM((2 * A_UNROLL, 2, cpb, 1, KB), F32),
                          pltpu.VMEM((2 * B_UNROLL, HG_BLK, HG_BLK), BF16)],
        compiler_params=_cparams(2),
        name="hgrn2",
    )(proj3, proj3, proj3, proj3, proj3, lb2, hnorm, tri_l, tri_u,
      tri_l.astype(F32), tri_u.astype(F32))


def _merge_kernel(x_ref, ua_ref, ub_ref, cq_ref, cz_ref, mkv_ref, ga_ref, gb_ref, gc_ref,
                  wb_ref, wo_ref, pn_ref, o_ref):
    def mem_head(c):
        cols = slice(c * DC, (c + 1) * DC)
        q = cq_ref[:, cols] * jnp.asarray(DC ** -0.5, BF16)
        s = lax.dot_general(q, mkv_ref[0, :, cols], (((1,), (1,)), ((), ())), preferred_element_type=F32)
        e = jnp.exp(s - jnp.max(s, axis=-1, keepdims=True))
        p = e / jnp.sum(e, axis=-1, keepdims=True)
        o = jnp.dot(p.astype(BF16), mkv_ref[0, :, HC * DC + c * DC:HC * DC + (c + 1) * DC],
                    preferred_element_type=F32)
        z = cz_ref[:, cols].astype(F32)
        return (o * (z * jax.nn.sigmoid(z))).astype(BF16)

    def branch(u, g_ref, idx):
        y = jnp.dot(u, wb_ref[idx], preferred_element_type=F32)
        return jax.nn.sigmoid(g_ref[...].astype(F32)) * y

    uc = jnp.concatenate([mem_head(c) for c in range(HC)], axis=1)
    merged = branch(ua_ref[...], ga_ref, 0) + branch(ub_ref[...], gb_ref, 1) + branch(uc, gc_ref, 2)
    y = jnp.dot(merged.astype(BF16), wo_ref[...], preferred_element_type=F32)
    ms = jnp.mean(y * y, axis=-1, keepdims=True)
    o_ref[...] = x_ref[...] + y * lax.rsqrt(ms + EPS) * pn_ref[...]


def _merge(x2d, ua, ub, proj2, mkv3, wb, wo, pn, tm, rows_per_batch):
    m, d = x2d.shape
    col = lambda c: pl.BlockSpec((tm, d), lambda i: (i, c // d))
    tiles_per_batch = rows_per_batch // tm
    return pl.pallas_call(
        _merge_kernel,
        grid=(m // tm,),
        in_specs=[col(0), col(0), col(0), col(COL_CQ), col(COL_CZ),
                  pl.BlockSpec((1, N_MEM, 2 * HC * DC), lambda i: (i // tiles_per_batch, 0, 0)),
                  col(COL_G), col(COL_G + d), col(COL_G + 2 * d),
                  pl.BlockSpec((3, d, d), lambda i: (0, 0, 0)),
                  pl.BlockSpec((d, d), lambda i: (0, 0)),
                  pl.BlockSpec((1, d), lambda i: (0, 0))],
        out_specs=col(0),
        out_shape=jax.ShapeDtypeStruct((m, d), F32),
        compiler_params=_cparams(1),
        name="merge_out",
    )(x2d, ua, ub, proj2, proj2, mkv3, proj2, proj2, proj2, wb, wo, pn)


def _t5_bucket(rel):
    nb = NUM_BUCKETS // 2
    max_exact = nb // 2
    ret = jnp.where(rel > 0, nb, 0)
    n = jnp.abs(rel)
    nf = jnp.maximum(n, 1).astype(jnp.float32)
    large = max_exact + (jnp.log(nf / max_exact) / math.log(MAX_DISTANCE / max_exact)
                         * (nb - max_exact)).astype(jnp.int32)
    large = jnp.minimum(large, nb - 1)
    return ret + jnp.where(n < max_exact, n, large)


def _bias_blocks(rel_bias):
    n, w = BIAS_BLK, 2 * BIAS_BLK
    assert n >= MAX_DISTANCE
    k = jnp.arange(w, dtype=jnp.int32)
    k = jnp.where(k < n, k, k - w)
    o = jnp.arange(-(N_BIAS // 2), N_BIAS // 2 + 1, dtype=jnp.int32)
    u = rel_bias.astype(F32)[_t5_bucket(n * o[:, None] - k[None, :])] * LOG2E
    u = jnp.transpose(u, (2, 0, 1))
    flat = jnp.tile(u, (1, 1, n))[:, :, :n * (w - 1)]
    return flat.reshape(HA, N_BIAS, n, w - 1)[:, :, :, :n]


def kernel(x, mem, pre_norm, post_norm, w_in, lambda_q1, lambda_k1, lambda_q2, lambda_k2,
           diff_subln, rel_bias, lb_logits, hgrn_norm, mem_norm, w_mem_kv, w_branch, w_out):
    b, l, d = x.shape
    t = b * l
    layer = 0
    att_tq, att_tk = min(ATT_TQ, l), min(ATT_TK, l)

    lam_init = 0.8 - 0.6 * math.exp(-0.3 * layer)
    lam = (jnp.exp(jnp.sum(lambda_q1[layer].astype(F32) * lambda_k1[layer].astype(F32)))
           - jnp.exp(jnp.sum(lambda_q2[layer].astype(F32) * lambda_k2[layer].astype(F32)))
           + lam_init)
    assert att_tq % att_tk == 0 and att_tk % BIAS_BLK == 0
    bias_blk = _bias_blocks(rel_bias)
    saturated = bias_blk[:, (0, N_BIAS - 1), 0, 0]
    scalars = jnp.concatenate([lam.reshape(1), saturated.reshape(-1)]).astype(F32)
    lb_all = jnp.cumsum(jax.nn.softmax(lb_logits.astype(F32), axis=1), axis=1)
    lb2 = lb_all[:, layer, :]
    w_in_p = w_in[layer].astype(BF16)
    blk = np.arange(HG_BLK)
    same = (blk[:, None] // CHUNK) == (blk[None, :] // CHUNK)
    tri_l = jnp.asarray(same & (blk[None, :] <= blk[:, None]), BF16)
    tri_u = jnp.asarray(same & (blk[None, :] >= blk[:, None]), BF16)

    proj2 = _norm_matmul(x.reshape(t, d), pre_norm[layer].reshape(1, d), w_in_p, tm=min(2048, t), tn=2048)
    proj3 = proj2.reshape(b, l, IN_COLS)

    ua = _diff_attention(proj3, scalars, bias_blk, diff_subln[layer].reshape(1, 2 * DA).astype(F32),
                         att_tq, att_tk, 1.0 - lam_init)

    ub = _hgrn(proj3, lb2, hgrn_norm[layer].reshape(1, VB).astype(F32), tri_l, tri_u)

    mkv = _norm_matmul(mem.reshape(b * N_MEM, d), mem_norm[layer].reshape(1, d),
                       w_mem_kv[layer].astype(BF16), tm=min(1024, b * N_MEM), tn=2048)
    tm = min(512, l)
    out = _merge(x.reshape(t, d), ua.reshape(t, d), ub.reshape(t, d), proj2,
                 mkv.reshape(b, N_MEM, 2 * HC * DC), w_branch[layer].astype(BF16), w_out[layer].astype(BF16),
                 post_norm[layer].reshape(1, d).astype(F32), tm=tm, rows_per_batch=l)
    return out.reshape(b, l, d)
```

```python
import functools
import math

import jax
import jax.numpy as jnp
import numpy as np
from jax import lax
from jax.experimental import pallas as pl
from jax.experimental.pallas import tpu as pltpu

F32 = jnp.float32
BF16 = jnp.bfloat16

D_MODEL = 1024
N_MEM = 256
HA, DA = 8, 64
HB, KB, VB = 8, 128, 128
HC, DC = 4, 256
NUM_BUCKETS, MAX_DISTANCE = 32, 128
CHUNK = 64
EPS = 1e-6
LANE = 128

COL_AQ, COL_AK, COL_AV, COL_AZ = 0, 1024, 2048, 3072
COL_BQ, COL_BFF, COL_BFB, COL_BI, COL_BZ = 4096, 5120, 6144, 7168, 8192
COL_CQ, COL_CZ, COL_G = 9216, 10240, 11264
IN_COLS = 14336

VMEM_LIMIT = 56 * 1024 * 1024


def _cparams(n_axes, flags=None):
    return pltpu.CompilerParams(dimension_semantics=("arbitrary",) * n_axes,
                                vmem_limit_bytes=VMEM_LIMIT, flags=flags)


def _norm_matmul_kernel(x_ref, nw_ref, w_ref, o_ref, h_ref):
    @pl.when(pl.program_id(1) == 0)
    def _():
        x = x_ref[...]
        ms = jnp.mean(x * x, axis=-1, keepdims=True)
        h_ref[...] = (x * lax.rsqrt(ms + EPS) * nw_ref[...]).astype(BF16)

    o_ref[...] = jnp.dot(h_ref[...], w_ref[...], preferred_element_type=F32).astype(o_ref.dtype)


def _norm_matmul(x2d, norm_w, w_bf16, tm, tn):
    m, d = x2d.shape
    n = w_bf16.shape[1]
    return pl.pallas_call(
        _norm_matmul_kernel,
        grid=(m // tm, n // tn),
        in_specs=[pl.BlockSpec((tm, d), lambda i, j: (i, 0)),
                  pl.BlockSpec((1, d), lambda i, j: (0, 0)),
                  pl.BlockSpec((d, tn), lambda i, j: (0, j))],
        out_specs=pl.BlockSpec((tm, tn), lambda i, j: (i, j)),
        out_shape=jax.ShapeDtypeStruct((m, n), BF16),
        scratch_shapes=[pltpu.VMEM((tm, d), BF16)],
        compiler_params=_cparams(2),
        name="norm_matmul",
    )(x2d, norm_w, w_bf16)


ATT_GROUP = 512
ATT_TQ, ATT_TK = 2048, 512
BIAS_BLK = 128
N_BIAS = 5
LOG2E = math.log2(math.e)
SUM_ROWS = 16


def _attn_kernel(sc_ref, q1_ref, q2_ref, k1_ref, k2_ref, v_ref, z_ref, bias_ref, sub_ref, o_ref,
                 vt_ref, qqt_ref, acc_ref, m_ref, s_ref, mx_ref, c_ref, p_ref, a_ref,
                 *, tq, tk, nk, out_scale):
    h = pl.program_id(1)
    i = pl.program_id(2)
    lam = sc_ref[0]
    ng = 2 * tq // ATT_GROUP
    gpm = tq // ATT_GROUP
    kblks, qblks = tk // BIAS_BLK, ATT_GROUP // BIAS_BLK

    @pl.when(i == 0)
    def _():
        for c in range(nk):
            vt_ref[c, 0:LANE] = v_ref[0, c * tk:(c + 1) * tk, :].astype(F32).T.astype(BF16)
            vt_ref[c, LANE:LANE + SUM_ROWS] = jnp.ones((SUM_ROWS, tk), BF16)

    row = lax.broadcasted_iota(jnp.int32, (LANE, ATT_GROUP), 0)
    keep = (row // DA) == (h % (LANE // DA))
    for mp, q_ref in enumerate((q1_ref, q2_ref)):
        qt = (q_ref[0].astype(F32) * (DA ** -0.5 * LOG2E)).T
        for g in range(gpm):
            piece = qt[:, g * ATT_GROUP:(g + 1) * ATT_GROUP]
            qqt_ref[mp * gpm + g] = jnp.where(keep, piece, 0.0).astype(BF16)

    m_ref[...] = jnp.full(m_ref.shape, -jnp.inf, F32)
    acc_ref[...] = jnp.zeros(acc_ref.shape, F32)

    half = N_BIAS // 2
    c_left, c_right = sc_ref[1 + 2 * h], sc_ref[2 + 2 * h]
    ratio = (tq // BIAS_BLK) // kblks

    def stage_a(j, g, mode):
        qb0 = (g % gpm) * qblks
        mx = None
        for kb in range(kblks):
            rows = pl.ds(pl.multiple_of(j * tk + kb * BIAS_BLK, BIAS_BLK), BIAS_BLK)
            k_ref = k1_ref if g < gpm else k2_ref
            s = jnp.dot(k_ref[0, rows, :], qqt_ref[g], preferred_element_type=F32)
            cols, consts = [], []
            for qb in range(qblks):
                blk = s[:, qb * BIAS_BLK:(qb + 1) * BIAS_BLK]
                if mode == "table":
                    o = j * kblks + kb - (i * (tq // BIAS_BLK) + qb0 + qb)
                    blk, c = blk + bias_ref[0, jnp.clip(o, -half, half) + half], 0.0
                elif mode == "left":
                    c = c_left
                elif mode == "right":
                    c = c_right
                else:
                    o = mode * kblks + kb - (qb0 + qb)
                    if abs(o) < half:
                        blk, c = blk + bias_ref[0, o + half], 0.0
                    else:
                        c = c_left if o < 0 else c_right
                cols.append(blk)
                consts.append(jnp.full((1, BIAS_BLK), c, F32))
            s = jnp.concatenate(cols, axis=1)
            cvec = jnp.concatenate(consts, axis=1)
            s_ref[g % 2, kb * BIAS_BLK:(kb + 1) * BIAS_BLK] = s
            c_ref[g % 2, kb] = cvec
            cur = jnp.max(s, axis=0, keepdims=True) + cvec
            mx = cur if mx is None else jnp.maximum(mx, cur)
        mx_ref[g % 2] = mx

    def stage_b(j, g):
        m_prev = m_ref[g]
        m_new = jnp.maximum(m_prev, mx_ref[g % 2])
        m_ref[g] = m_new
        a_ref[g % 2] = jnp.exp2(m_prev - m_new)
        for kb in range(kblks):
            rows = slice(kb * BIAS_BLK, (kb + 1) * BIAS_BLK)
            p_ref[g % 2, rows] = jnp.exp2(s_ref[g % 2, rows] - (m_new - c_ref[g % 2, kb])).astype(BF16)

    def stage_c(j, g):
        acc_ref[g] = a_ref[g % 2] * acc_ref[g] + jnp.dot(vt_ref[j], p_ref[g % 2],
                                                         preferred_element_type=F32)

    def item_before(j, g, back):
        return (j, g - back) if g >= back else (j - 1, g - back + ng)

    def steps(j, mode, first=False):
        for g in range(ng):
            if not (first and g < 2):
                stage_c(*item_before(j, g, 2))
            if not (first and g < 1):
                stage_b(*item_before(j, g, 1))
            stage_a(j, g, mode)

    def tiles(lo, hi, mode):
        def body(j, carry):
            steps(j, mode)
            return carry
        lax.fori_loop(lo, hi, body, 0)

    steps(0, "table", first=True)
    tiles(1, jnp.maximum(ratio * i - 1, 1), "left")
    for w in range(-1, ratio + 1):
        j_w = ratio * i + w

        @pl.when((j_w >= 1) & (j_w < nk))
        def _():
            steps(j_w, w)
    tiles(jnp.maximum(ratio * i + ratio + 1, 1), nk, "right")
    stage_c(nk - 1, ng - 2)
    stage_b(nk - 1, ng - 1)
    stage_c(nk - 1, ng - 1)

    def normalized(g):
        acc = acc_ref[g]
        return acc[0:LANE] / acc[LANE:LANE + 1]

    d = jnp.concatenate([normalized(g) - lam * normalized(g + gpm) for g in range(gpm)],
                        axis=1).T
    ms = jnp.mean(d * d, axis=-1, keepdims=True)
    y = d * lax.rsqrt(ms + EPS) * sub_ref[...] * out_scale
    z = z_ref[0].astype(F32)
    o_ref[0] = (y * (z * jax.nn.sigmoid(z))).astype(o_ref.dtype)


def _diff_attention(proj3, scalars, bias_blk, subln, tq, tk, out_scale):
    b, l, _ = proj3.shape
    nk = l // tk
    ng = 2 * tq // ATT_GROUP
    cq, ck, cv, cz = COL_AQ // LANE, COL_AK // LANE, COL_AV // LANE, COL_AZ // LANE
    hpb = LANE // DA
    mapw = HA * DA // LANE
    kern = functools.partial(_attn_kernel, tq=tq, tk=tk, nk=nk, out_scale=out_scale)
    return pl.pallas_call(
        kern,
        grid=(b, HA, l // tq),
        in_specs=[pl.BlockSpec(memory_space=pltpu.SMEM),
                  pl.BlockSpec((1, tq, LANE), lambda b_, h, i: (b_, i, cq + h // hpb)),
                  pl.BlockSpec((1, tq, LANE), lambda b_, h, i: (b_, i, cq + mapw + h // hpb)),
                  pl.BlockSpec((1, l, LANE), lambda b_, h, i: (b_, 0, ck + h // hpb)),
                  pl.BlockSpec((1, l, LANE), lambda b_, h, i: (b_, 0, ck + mapw + h // hpb)),
                  pl.BlockSpec((1, l, LANE), lambda b_, h, i: (b_, 0, cv + h)),
                  pl.BlockSpec((1, tq, LANE), lambda b_, h, i: (b_, i, cz + h)),
                  pl.BlockSpec((1, N_BIAS, BIAS_BLK, BIAS_BLK), lambda b_, h, i: (h, 0, 0, 0)),
                  pl.BlockSpec((1, LANE), lambda b_, h, i: (0, 0))],
        out_specs=pl.BlockSpec((1, tq, LANE), lambda b_, h, i: (b_, i, h)),
        out_shape=jax.ShapeDtypeStruct((b, l, HA * 2 * DA), BF16),
        scratch_shapes=[pltpu.VMEM((nk, LANE + SUM_ROWS, tk), BF16),
                        pltpu.VMEM((ng, LANE, ATT_GROUP), BF16),
                        pltpu.VMEM((ng, LANE + SUM_ROWS, ATT_GROUP), F32),
                        pltpu.VMEM((ng, 1, ATT_GROUP), F32),
                        pltpu.VMEM((2, tk, ATT_GROUP), F32),
                        pltpu.VMEM((2, 1, ATT_GROUP), F32),
                        pltpu.VMEM((2, tk // BIAS_BLK, 1, ATT_GROUP), F32),
                        pltpu.VMEM((2, tk, ATT_GROUP), BF16),
                        pltpu.VMEM((2, 1, ATT_GROUP), F32)],
        compiler_params=_cparams(3),
        name="diff_attention",
    )(scalars, proj3, proj3, proj3, proj3, proj3, proj3, bias_blk, subln)


HG_BLK = 256
A_UNROLL, B_UNROLL = 2, 8


def _hgrn_kernel(q_ref, ff_ref, fb_ref, v_ref, z_ref, lb_ref, nw_ref, tl_ref, tu_ref, ml_ref, mu_ref, o_ref,
                 qtf_ref, ktf_ref, qtb_ref, ktb_ref, qh_ref, st_ref, sf_ref, sb_ref,
                 kh_ref, dec_ref, a_ref, *, l):
    nblk = l // HG_BLK
    cpb = HG_BLK // CHUNK
    tn = (((0,), (0,)), ((), ()))
    nt = (((1,), (1,)), ((), ()))
    dirs = ((ff_ref, 0, tl_ref, CHUNK // 2 - 1, CHUNK - 1, qtf_ref, ktf_ref, sf_ref),
            (fb_ref, 1, tu_ref, CHUNK // 2, 0, qtb_ref, ktb_ref, sb_ref))

    def block_of(r, di):
        return r if di == 0 else nblk - 1 - r

    def cumsum(tri, g):
        g_hi = g.astype(BF16)
        g_lo = (g - g_hi.astype(F32)).astype(BF16)
        return (jnp.dot(tri, g_hi, preferred_element_type=F32)
                + jnp.dot(tri, g_lo, preferred_element_type=F32))

    def prep(r):
        for f_ref, di, tri_ref, mid_row, last_row, qt_ref, kt_ref, _ in dirs:
            rows = pl.ds(pl.multiple_of(block_of(r, di) * HG_BLK, HG_BLK), HG_BLK)
            lb = lb_ref[di:di + 1, :]
            x = f_ref[0, rows, :].astype(F32)
            q3 = q_ref[0, rows, :].astype(F32).reshape(cpb, CHUNK, LANE)
            f = lb + (1.0 - lb) * jax.nn.sigmoid(x)
            g = jnp.log(f)
            k3 = (1.0 - f).reshape(cpb, CHUNK, LANE)
            bcs = cumsum(tri_ref[...], g).reshape(cpb, CHUNK, LANE)
            b_mid = bcs[:, mid_row:mid_row + 1, :]
            b_last = bcs[:, last_row:last_row + 1, :]
            qt = q3 * jnp.exp(bcs - b_mid)
            kt = k3 * jnp.exp(b_mid - bcs)
            qh = qt * jnp.exp(b_mid)
            kh = kt * jnp.exp(b_last - b_mid)
            qt_ref[rows, :] = qt.reshape(HG_BLK, LANE).astype(BF16)
            kt_ref[rows, :] = kt.reshape(HG_BLK, LANE).astype(BF16)
            qh_ref[rows, di * KB:(di + 1) * KB] = qh.reshape(HG_BLK, LANE).astype(BF16)
            kh_ref[r % (2 * A_UNROLL), di] = kh.reshape(HG_BLK, LANE).astype(BF16)
            dec_ref[r % (2 * A_UNROLL), di] = jnp.exp(b_last)

    def scan(r):
        for _, di, _, _, _, _, _, s_ref in dirs:
            blk = block_of(r, di)
            s = s_ref[...]
            for c in (range(cpb) if di == 0 else reversed(range(cpb))):
                st_ref[blk * cpb + c, :, di * KB:(di + 1) * KB] = s.astype(BF16)
                vrows = pl.ds(pl.multiple_of(blk * HG_BLK + c * CHUNK, CHUNK), CHUNK)
                u = lax.dot_general(v_ref[0, vrows, :],
                                    kh_ref[r % (2 * A_UNROLL), di, c * CHUNK:(c + 1) * CHUNK, :], tn,
                                    preferred_element_type=F32)
                s = s * dec_ref[r % (2 * A_UNROLL), di, c] + u
            s_ref[...] = s

    def pipelined(first, second, unroll):
        def group(fn, gi):
            for u in range(unroll):
                fn(gi * unroll + u)

        group(first, 0)

        def body(gi, carry):
            group(second, gi - 1)
            group(first, gi)
            return carry

        lax.fori_loop(1, nblk // unroll, body, 0)
        group(second, nblk // unroll - 1)

    sf_ref[...] = jnp.zeros(sf_ref.shape, F32)
    sb_ref[...] = jnp.zeros(sb_ref.shape, F32)
    pipelined(prep, scan, A_UNROLL)

    nw = nw_ref[...]

    def scores(r):
        rows = pl.ds(pl.multiple_of(r * HG_BLK, HG_BLK), HG_BLK)
        a_f = lax.dot_general(qtf_ref[rows, :], ktf_ref[rows, :], nt, preferred_element_type=F32)
        a_b = lax.dot_general(qtb_ref[rows, :], ktb_ref[rows, :], nt, preferred_element_type=F32)
        a = jnp.where(ml_ref[...] != 0.0, a_f, 0.0) + jnp.where(mu_ref[...] != 0.0, a_b, 0.0)
        a_ref[r % (2 * B_UNROLL)] = a.astype(BF16)

    def outputs(r):
        rows = pl.ds(pl.multiple_of(r * HG_BLK, HG_BLK), HG_BLK)
        inter = jnp.concatenate(
            [lax.dot_general(qh_ref[pl.ds(pl.multiple_of(r * HG_BLK + c * CHUNK, CHUNK), CHUNK), :],
                             st_ref[r * cpb + c], nt, preferred_element_type=F32) for c in range(cpb)], axis=0)
        o = jnp.dot(a_ref[r % (2 * B_UNROLL)], v_ref[0, rows, :], preferred_element_type=F32) + inter
        ms = jnp.mean(o * o, axis=-1, keepdims=True)
        y = o * lax.rsqrt(ms + EPS) * nw
        z = z_ref[0, rows, :].astype(F32)
        o_ref[0, rows, :] = (y * (z * jax.nn.sigmoid(z))).astype(o_ref.dtype)

    pipelined(scores, outputs, B_UNROLL)


def _hgrn(proj3, lb2, hnorm, tri_l, tri_u):
    b, l, _ = proj3.shape
    assert l % (HG_BLK * A_UNROLL) == 0 and l % (HG_BLK * B_UNROLL) == 0
    n = l // CHUNK
    cpb = HG_BLK // CHUNK
    c = lambda col: col // LANE
    seq = lambda col: pl.BlockSpec((1, l, LANE), lambda b_, h: (b_, 0, c(col) + h))
    blk = lambda: pl.BlockSpec((HG_BLK, HG_BLK), lambda b_, h: (0, 0))
    return pl.pallas_call(
        functools.partial(_hgrn_kernel, l=l),
        grid=(b, HB),
        in_specs=[seq(COL_BQ), seq(COL_BFF), seq(COL_BFB), seq(COL_BI), seq(COL_BZ),
                  pl.BlockSpec((2, LANE), lambda b_, h: (0, h)),
                  pl.BlockSpec((1, LANE), lambda b_, h: (0, 0)),
                  blk(), blk(), blk(), blk()],
        out_specs=pl.BlockSpec((1, l, LANE), lambda b_, h: (b_, 0, h)),
        out_shape=jax.ShapeDtypeStruct((b, l, HB * VB), BF16),
        scratch_shapes=[pltpu.VMEM((l, KB), BF16)] * 4
                       + [pltpu.VMEM((l, 2 * KB), BF16),
                          pltpu.VMEM((n, VB, 2 * KB), BF16),
                          pltpu.VMEM((VB, KB), F32), pltpu.VMEM((VB, KB), F32),
                          pltpu.VMEM((2 * A_UNROLL, 2, HG_BLK, KB), BF16),
                          pltpu.VMEM((2 * A_UNROLL, 2, cpb, 1, KB), F32),
                          pltpu.VMEM((2 * B_UNROLL, HG_BLK, HG_BLK), BF16)],
        compiler_params=_cparams(2),
        name="hgrn2",
    )(proj3, proj3, proj3, proj3, proj3, lb2, hnorm, tri_l, tri_u,
      tri_l.astype(F32), tri_u.astype(F32))


def _merge_kernel(x_ref, ua_ref, ub_ref, cq_ref, cz_ref, mkv_ref, ga_ref, gb_ref, gc_ref,
                  wb_ref, wo_ref, pn_ref, o_ref):
    def mem_head(c):
        cols = slice(c * DC, (c + 1) * DC)
        q = cq_ref[:, cols] * jnp.asarray(DC ** -0.5, BF16)
        s = lax.dot_general(q, mkv_ref[0, :, cols], (((1,), (1,)), ((), ())), preferred_element_type=F32)
        e = jnp.exp(s - jnp.max(s, axis=-1, keepdims=True))
        p = e / jnp.sum(e, axis=-1, keepdims=True)
        o = jnp.dot(p.astype(BF16), mkv_ref[0, :, HC * DC + c * DC:HC * DC + (c + 1) * DC],
                    preferred_element_type=F32)
        z = cz_ref[:, cols].astype(F32)
        return (o * (z * jax.nn.sigmoid(z))).astype(BF16)

    def branch(u, g_ref, idx):
        y = jnp.dot(u, wb_ref[idx], preferred_element_type=F32)
        return jax.nn.sigmoid(g_ref[...].astype(F32)) * y

    uc = jnp.concatenate([mem_head(c) for c in range(HC)], axis=1)
    merged = branch(ua_ref[...], ga_ref, 0) + branch(ub_ref[...], gb_ref, 1) + branch(uc, gc_ref, 2)
    y = jnp.dot(merged.astype(BF16), wo_ref[...], preferred_element_type=F32)
    ms = jnp.mean(y * y, axis=-1, keepdims=True)
    o_ref[...] = x_ref[...] + y * lax.rsqrt(ms + EPS) * pn_ref[...]


def _merge(x2d, ua, ub, proj2, mkv3, wb, wo, pn, tm, rows_per_batch):
    m, d = x2d.shape
    col = lambda c: pl.BlockSpec((tm, d), lambda i: (i, c // d))
    tiles_per_batch = rows_per_batch // tm
    return pl.pallas_call(
        _merge_kernel,
        grid=(m // tm,),
        in_specs=[col(0), col(0), col(0), col(COL_CQ), col(COL_CZ),
                  pl.BlockSpec((1, N_MEM, 2 * HC * DC), lambda i: (i // tiles_per_batch, 0, 0)),
                  col(COL_G), col(COL_G + d), col(COL_G + 2 * d),
                  pl.BlockSpec((3, d, d), lambda i: (0, 0, 0)),
                  pl.BlockSpec((d, d), lambda i: (0, 0)),
                  pl.BlockSpec((1, d), lambda i: (0, 0))],
        out_specs=col(0),
        out_shape=jax.ShapeDtypeStruct((m, d), F32),
        compiler_params=_cparams(1),
        name="merge_out",
    )(x2d, ua, ub, proj2, proj2, mkv3, proj2, proj2, proj2, wb, wo, pn)


def _t5_bucket(rel):
    nb = NUM_BUCKETS // 2
    max_exact = nb // 2
    ret = jnp.where(rel > 0, nb, 0)
    n = jnp.abs(rel)
    nf = jnp.maximum(n, 1).astype(jnp.float32)
    large = max_exact + (jnp.log(nf / max_exact) / math.log(MAX_DISTANCE / max_exact)
                         * (nb - max_exact)).astype(jnp.int32)
    large = jnp.minimum(large, nb - 1)
    return ret + jnp.where(n < max_exact, n, large)


def _bias_blocks(rel_bias):
    n, w = BIAS_BLK, 2 * BIAS_BLK
    assert n >= MAX_DISTANCE
    k = jnp.arange(w, dtype=jnp.int32)
    k = jnp.where(k < n, k, k - w)
    o = jnp.arange(-(N_BIAS // 2), N_BIAS // 2 + 1, dtype=jnp.int32)
    u = rel_bias.astype(F32)[_t5_bucket(n * o[:, None] - k[None, :])] * LOG2E
    u = jnp.transpose(u, (2, 0, 1))
    flat = jnp.tile(u, (1, 1, n))[:, :, :n * (w - 1)]
    return flat.reshape(HA, N_BIAS, n, w - 1)[:, :, :, :n]


def kernel(x, mem, pre_norm, post_norm, w_in, lambda_q1, lambda_k1, lambda_q2, lambda_k2,
           diff_subln, rel_bias, lb_logits, hgrn_norm, mem_norm, w_mem_kv, w_branch, w_out):
    b, l, d = x.shape
    t = b * l
    layer = 0
    att_tq, att_tk = min(ATT_TQ, l), min(ATT_TK, l)

    lam_init = 0.8 - 0.6 * math.exp(-0.3 * layer)
    lam = (jnp.exp(jnp.sum(lambda_q1[layer].astype(F32) * lambda_k1[layer].astype(F32)))
           - jnp.exp(jnp.sum(lambda_q2[layer].astype(F32) * lambda_k2[layer].astype(F32)))
           + lam_init)
    assert att_tq % att_tk == 0 and att_tk % BIAS_BLK == 0
    bias_blk = _bias_blocks(rel_bias)
    saturated = bias_blk[:, (0, N_BIAS - 1), 0, 0]
    scalars = jnp.concatenate([lam.reshape(1), saturated.reshape(-1)]).astype(F32)
    lb_all = jnp.cumsum(jax.nn.softmax(lb_logits.astype(F32), axis=1), axis=1)
    lb2 = lb_all[:, layer, :]
    w_in_p = w_in[layer].astype(BF16)
    blk = np.arange(HG_BLK)
    same = (blk[:, None] // CHUNK) == (blk[None, :] // CHUNK)
    tri_l = jnp.asarray(same & (blk[None, :] <= blk[:, None]), BF16)
    tri_u = jnp.asarray(same & (blk[None, :] >= blk[:, None]), BF16)

    proj2 = _norm_matmul(x.reshape(t, d), pre_norm[layer].reshape(1, d), w_in_p, tm=min(2048, t), tn=2048)
    proj3 = proj2.reshape(b, l, IN_COLS)

    ua = _diff_attention(proj3, scalars, bias_blk, diff_subln[layer].reshape(1, 2 * DA).astype(F32),
                         att_tq, att_tk, 1.0 - lam_init)

    ub = _hgrn(proj3, lb2, hgrn_norm[layer].reshape(1, VB).astype(F32), tri_l, tri_u)

    mkv = _norm_matmul(mem.reshape(b * N_MEM, d), mem_norm[layer].reshape(1, d),
                       w_mem_kv[layer].astype(BF16), tm=min(1024, b * N_MEM), tn=2048)
    tm = min(512, l)
    out = _merge(x.reshape(t, d), ua.reshape(t, d), ub.reshape(t, d), proj2,
                 mkv.reshape(b, N_MEM, 2 * HC * DC), w_branch[layer].astype(BF16), w_out[layer].astype(BF16),
                 post_norm[layer].reshape(1, d).astype(F32), tm=tm, rows_per_batch=l)
    return out.reshape(b, l, d)
```
